```python
import jax, jax.numpy as jnp
from jax import lax
import numpy as np

D_MODEL = 2048
BATCH = 4
SEQ = 2048
DEPTH = 4
DEC_BATCH = 8
DEC_SEQ = 4
PAST_LEN = 16384
PAGE_SIZE = 128

N_MIXERS = 2
N_GLA_LAYERS = (DEPTH + 1) // 2
N_MOBA_LAYERS = DEPTH // 2
NORM_EPS = 1e-6

GLA_HEADS = 4
GLA_KEY_DIM = D_MODEL // 2
GLA_VALUE_DIM = D_MODEL
GLA_DK = GLA_KEY_DIM // GLA_HEADS
GLA_DV = GLA_VALUE_DIM // GLA_HEADS
GLA_GATE_RANK = 16
GLA_GATE_NORMALIZER = 16.0
GLA_CHUNK = 64
GLA_IN_WIDTH = 2 * GLA_KEY_DIM + 2 * GLA_VALUE_DIM + GLA_GATE_RANK

MOBA_HEADS = 16
MOBA_HEAD_DIM = D_MODEL // MOBA_HEADS
MOBA_BLOCK = 256
MOBA_TOP_K = 3
MOBA_Q_CHUNK = 8
ROPE_THETA = 500000.0
ROPE_DIM = MOBA_HEAD_DIM // 4

FFN_HIDDEN = ((8 * D_MODEL // 3 + 255) // 256) * 256

kernel_name = "hybrid_gla_moba_decode_step"


def rmsnorm(x, g):
    xf = x.astype(jnp.float32)
    y = xf * lax.rsqrt(jnp.mean(xf * xf, axis=-1, keepdims=True) + NORM_EPS)
    return (y * g.astype(jnp.float32)).astype(x.dtype)


def partial_rope(x, pos):
    half = ROPE_DIM // 2
    inv = ROPE_THETA ** (-jnp.arange(half, dtype=jnp.float32) / half)
    ang = pos.astype(jnp.float32)[:, None] * inv[None, :]
    cos = jnp.cos(ang)[None, :, None, :]
    sin = jnp.sin(ang)[None, :, None, :]
    xr = x[..., :ROPE_DIM].astype(jnp.float32)
    x1, x2 = xr[..., :half], xr[..., half:]
    rot = jnp.concatenate([x1 * cos - x2 * sin, x2 * cos + x1 * sin], axis=-1).astype(x.dtype)
    return jnp.concatenate([rot, x[..., ROPE_DIM:]], axis=-1)


def gla_chunk_scan(q, k, v, g, s0, chunk):
    B, H, S, _ = q.shape
    nc = S // chunk

    def split(t):
        return jnp.moveaxis(t.reshape(B, H, nc, chunk, t.shape[-1]), 2, 0)

    causal = jnp.tril(jnp.ones((chunk, chunk), dtype=bool))[:, :, None]

    def step(state, inp):
        qc, kc, vc, gc = inp
        b = jnp.cumsum(gc, axis=2)
        o_inter = jnp.einsum('bhtk,bhkv->bhtv', qc * jnp.exp(b), state)
        diff = b[:, :, :, None, :] - b[:, :, None, :, :]
        decay = jnp.exp(jnp.where(causal, diff, -jnp.inf))
        attn = jnp.einsum('bhtk,bhjk,bhtjk->bhtj', qc, kc, decay)
        o = o_inter + jnp.einsum('bhtj,bhjv->bhtv', attn, vc)
        b_last = b[:, :, -1:, :]
        new_state = (jnp.exp(b_last[:, :, 0, :])[..., None] * state
                     + jnp.einsum('bhjk,bhjv->bhkv', kc * jnp.exp(b_last - b), vc))
        return new_state, o

    s_fin, o = lax.scan(step, s0, (split(q), split(k), split(v), split(g)))
    o = jnp.moveaxis(o, 0, 2).reshape(B, H, S, -1)
    return o, s_fin


def gla_mixer(h, s0, w_in, w_gate, b_gate, head_norm, w_out, chunk):
    B, S, _ = h.shape
    proj = h @ w_in
    c1 = GLA_KEY_DIM
    c2 = 2 * GLA_KEY_DIM
    c3 = c2 + GLA_VALUE_DIM
    c4 = c3 + GLA_VALUE_DIM
    q, k, v, r, glr = proj[..., :c1], proj[..., c1:c2], proj[..., c2:c3], proj[..., c3:c4], proj[..., c4:]
    g = jax.nn.log_sigmoid((glr @ w_gate + b_gate).astype(jnp.float32)) / GLA_GATE_NORMALIZER

    def heads(t, d):
        return jnp.moveaxis(t.reshape(B, S, GLA_HEADS, d), 2, 1).astype(jnp.float32)

    o, s_fin = gla_chunk_scan(heads(q, GLA_DK) * (GLA_DK ** -0.5), heads(k, GLA_DK),
                              heads(v, GLA_DV), heads(g, GLA_DK), s0.astype(jnp.float32), chunk)
    o = jnp.moveaxis(o, 1, 2)
    o = rmsnorm(o, head_norm) * jax.nn.silu(r.reshape(B, S, GLA_HEADS, GLA_DV).astype(jnp.float32))
    y = o.reshape(B, S, GLA_VALUE_DIM).astype(h.dtype) @ w_out
    return y, s_fin


def moba_attend(q, k_full, v_full, pos, q_chunk):
    B, S, H, hd = q.shape
    nb = k_full.shape[1] // MOBA_BLOCK
    k_blocks = k_full.reshape(B, nb, MOBA_BLOCK, H, hd)
    v_blocks = v_full.reshape(B, nb, MOBA_BLOCK, H, hd)
    k_mean = jnp.mean(k_blocks.astype(jnp.float32), axis=2)
    own = pos // MOBA_BLOCK
    gate = jnp.einsum('bshd,bnhd->bhsn', q.astype(jnp.float32), k_mean)
    past_ok = jnp.arange(nb)[None, :] < own[:, None]
    gate = jnp.where(past_ok[None, None], gate, -jnp.inf)
    n_top = min(MOBA_TOP_K, nb)
    _, top_idx = lax.top_k(gate, n_top)
    sel = jnp.concatenate([top_idx.astype(jnp.int32),
                           jnp.broadcast_to(own[None, None, :, None], (B, H, S, 1)).astype(jnp.int32)], axis=-1)
    n_sel = n_top + 1
    rank_ok = jnp.concatenate([jnp.arange(n_top)[None, :] < own[:, None],
                               jnp.ones((S, 1), dtype=bool)], axis=-1)
    n_ch = S // q_chunk
    xs = (jnp.moveaxis(q.reshape(B, n_ch, q_chunk, H, hd), 1, 0),
          jnp.moveaxis(sel.reshape(B, H, n_ch, q_chunk, n_sel), 2, 0),
          rank_ok.reshape(n_ch, q_chunk, n_sel),
          pos.reshape(n_ch, q_chunk))
    b_i = jnp.arange(B)[:, None, None, None]
    h_i = jnp.arange(H)[None, :, None, None]
    offs = jnp.arange(MOBA_BLOCK, dtype=jnp.int32)
    scale = hd ** -0.5

    def attend_chunk(inp):
        qc, selc, okc, posc = inp
        kg = k_blocks[b_i, selc, :, h_i, :]
        vg = v_blocks[b_i, selc, :, h_i, :]
        key_pos = selc[..., None] * MOBA_BLOCK + offs
        mask = okc[None, None, :, :, None] & (key_pos <= posc[None, None, :, None, None])
        s = jnp.einsum('bqhd,bhqnjd->bhqnj', qc, kg).astype(jnp.float32) * scale
        s = jnp.where(mask, s, -jnp.inf)
        p = jax.nn.softmax(s.reshape(B, H, q_chunk, -1), axis=-1).reshape(s.shape).astype(vg.dtype)
        return jnp.einsum('bhqnj,bhqnjd->bqhd', p, vg)

    o = lax.map(attend_chunk, xs)
    return jnp.moveaxis(o, 0, 1).reshape(B, S, H, hd)


def extend_rows(past, new, pad):
    parts = ([] if past is None else [past]) + [new]
    if pad > 0:
        parts.append(jnp.zeros((new.shape[0], pad) + new.shape[2:], new.dtype))
    return jnp.concatenate(parts, axis=1) if len(parts) > 1 else new


def moba_mixer(h, k_past, v_past, pos, w_qkv, w_o, q_chunk):
    B, S, _ = h.shape
    qkv = h @ w_qkv
    q = qkv[..., :D_MODEL].reshape(B, S, MOBA_HEADS, MOBA_HEAD_DIM)
    k = qkv[..., D_MODEL:2 * D_MODEL].reshape(B, S, MOBA_HEADS, MOBA_HEAD_DIM)
    v = qkv[..., 2 * D_MODEL:].reshape(B, S, MOBA_HEADS, MOBA_HEAD_DIM)
    q = partial_rope(q, pos)
    k = partial_rope(k, pos)
    past_len = 0 if k_past is None else k_past.shape[1]
    total = past_len + S
    nb = -(-total // MOBA_BLOCK)
    pad = nb * MOBA_BLOCK - total
    k_full = extend_rows(k_past, k, pad)
    v_full = extend_rows(v_past, v, pad)
    o = moba_attend(q, k_full, v_full, pos, q_chunk)
    y = o.reshape(B, S, D_MODEL) @ w_o
    return y, k, v


def swiglu(h, w_gate_up, w_down):
    gu = h @ w_gate_up
    return (jax.nn.silu(gu[..., :FFN_HIDDEN]) * gu[..., FFN_HIDDEN:]) @ w_down


def setup_inputs(seed: int = 0) -> dict:
    key = jax.random.key(seed)
    ks = jax.random.split(key, 20)
    n_pages = PAST_LEN // PAGE_SIZE
    n_used = DEC_BATCH * n_pages
    n_pool = n_used + n_used // 4
    nrm = jax.random.normal
    f32 = jnp.float32
    x_prompt = nrm(ks[0], (BATCH, SEQ, D_MODEL), f32)
    x_sample = nrm(ks[1], (DEC_BATCH, DEC_SEQ, D_MODEL), f32)
    state_gla = 0.5 * nrm(ks[2], (N_GLA_LAYERS, DEC_BATCH, GLA_HEADS, GLA_DK, GLA_DV), f32)
    cache_k = nrm(ks[3], (N_MOBA_LAYERS, n_pool, PAGE_SIZE, MOBA_HEADS, MOBA_HEAD_DIM), f32)
    cache_v = nrm(ks[4], (N_MOBA_LAYERS, n_pool, PAGE_SIZE, MOBA_HEADS, MOBA_HEAD_DIM), f32)
    page_table = jax.random.permutation(ks[5], n_pool)[:n_used].reshape(DEC_BATCH, n_pages).astype(jnp.int32)
    norm_mix = 1.0 + 0.02 * nrm(ks[6], (DEPTH, D_MODEL), f32)
    norm_ffn = 1.0 + 0.02 * nrm(ks[7], (DEPTH, D_MODEL), f32)
    norm_final = 1.0 + 0.02 * nrm(ks[8], (D_MODEL,), f32)
    gla_w_in = nrm(ks[9], (N_GLA_LAYERS, D_MODEL, GLA_IN_WIDTH), f32) * D_MODEL ** -0.5
    gla_w_gate = nrm(ks[10], (N_GLA_LAYERS, GLA_GATE_RANK, GLA_KEY_DIM), f32) * GLA_GATE_RANK ** -0.5
    gla_b_gate = 0.02 * nrm(ks[11], (N_GLA_LAYERS, GLA_KEY_DIM), f32)
    gla_head_norm = 1.0 + 0.02 * nrm(ks[12], (N_GLA_LAYERS, GLA_DV), f32)
    gla_w_out = nrm(ks[13], (N_GLA_LAYERS, GLA_VALUE_DIM, D_MODEL), f32) * GLA_VALUE_DIM ** -0.5
    moba_w_qkv = nrm(ks[14], (N_MOBA_LAYERS, D_MODEL, 3 * D_MODEL), f32) * D_MODEL ** -0.5
    moba_w_o = nrm(ks[15], (N_MOBA_LAYERS, D_MODEL, D_MODEL), f32) * D_MODEL ** -0.5
    ffn_w_gate_up = nrm(ks[16], (DEPTH, D_MODEL, 2 * FFN_HIDDEN), f32) * D_MODEL ** -0.5
    ffn_w_down = nrm(ks[17], (DEPTH, FFN_HIDDEN, D_MODEL), f32) * FFN_HIDDEN ** -0.5
    return {"x_prompt": x_prompt, "x_sample": x_sample, "state_gla": state_gla,
            "cache_k": cache_k, "cache_v": cache_v, "page_table": page_table,
            "norm_mix": norm_mix, "norm_ffn": norm_ffn, "norm_final": norm_final,
            "gla_w_in": gla_w_in, "gla_w_gate": gla_w_gate, "gla_b_gate": gla_b_gate,
            "gla_head_norm": gla_head_norm, "gla_w_out": gla_w_out,
            "moba_w_qkv": moba_w_qkv, "moba_w_o": moba_w_o,
            "ffn_w_gate_up": ffn_w_gate_up, "ffn_w_down": ffn_w_down}


def reference(x_prompt, x_sample, state_gla, cache_k, cache_v, page_table,
              norm_mix, norm_ffn, norm_final, gla_w_in, gla_w_gate, gla_b_gate,
              gla_head_norm, gla_w_out, moba_w_qkv, moba_w_o, ffn_w_gate_up, ffn_w_down):
    Bp, S, _ = x_prompt.shape
    Bd, Sd, _ = x_sample.shape
    n_pages = page_table.shape[1]
    past_len = n_pages * PAGE_SIZE
    pos_prompt = jnp.arange(S, dtype=jnp.int32)
    pos_sample = past_len + jnp.arange(Sd, dtype=jnp.int32)
    gla_chunk_p = GLA_CHUNK if S % GLA_CHUNK == 0 else S
    moba_chunk_p = MOBA_Q_CHUNK if S % MOBA_Q_CHUNK == 0 else S

    xp, xs = x_prompt, x_sample
    gla_sp, gla_ss, kp_l, vp_l, ks_l, vs_l = [], [], [], [], [], []
    for i in range(DEPTH):
        j = i // N_MIXERS
        hp = rmsnorm(xp, norm_mix[i])
        hs = rmsnorm(xs, norm_mix[i])
        if i % N_MIXERS == 0:
            s0p = jnp.zeros((Bp, GLA_HEADS, GLA_DK, GLA_DV), jnp.float32)
            yp, sp = gla_mixer(hp, s0p, gla_w_in[j], gla_w_gate[j], gla_b_gate[j],
                               gla_head_norm[j], gla_w_out[j], gla_chunk_p)
            ys, ss = gla_mixer(hs, state_gla[j], gla_w_in[j], gla_w_gate[j], gla_b_gate[j],
                               gla_head_norm[j], gla_w_out[j], Sd)
            gla_sp.append(sp)
            gla_ss.append(ss)
        else:
            k_past = cache_k[j, page_table].reshape(Bd, past_len, MOBA_HEADS, MOBA_HEAD_DIM)
            v_past = cache_v[j, page_table].reshape(Bd, past_len, MOBA_HEADS, MOBA_HEAD_DIM)
            yp, kpn, vpn = moba_mixer(hp, None, None, pos_prompt, moba_w_qkv[j], moba_w_o[j], moba_chunk_p)
            ys, ksn, vsn = moba_mixer(hs, k_past, v_past, pos_sample, moba_w_qkv[j], moba_w_o[j], Sd)
            kp_l.append(kpn.reshape(Bp * S // PAGE_SIZE, PAGE_SIZE, MOBA_HEADS, MOBA_HEAD_DIM))
            vp_l.append(vpn.reshape(Bp * S // PAGE_SIZE, PAGE_SIZE, MOBA_HEADS, MOBA_HEAD_DIM))
            ks_l.append(ksn)
            vs_l.append(vsn)
        xp = xp + yp
        xs = xs + ys
        xp = xp + swiglu(rmsnorm(xp, norm_ffn[i]), ffn_w_gate_up[i], ffn_w_down[i])
        xs = xs + swiglu(rmsnorm(xs, norm_ffn[i]), ffn_w_gate_up[i], ffn_w_down[i])

    y_prompt = rmsnorm(xp, norm_final)
    y_sample = rmsnorm(xs, norm_final)
    gla_state_prompt = jnp.stack(gla_sp, axis=0)
    gla_state_sample = jnp.stack(gla_ss, axis=0)
    k_prompt_pages = jnp.stack(kp_l, axis=0)
    v_prompt_pages = jnp.stack(vp_l, axis=0)
    k_sample_rows = jnp.stack(ks_l, axis=0)
    v_sample_rows = jnp.stack(vs_l, axis=0)
    return (y_prompt, y_sample, gla_state_prompt, gla_state_sample,
            k_prompt_pages, v_prompt_pages, k_sample_rows, v_sample_rows)
```

```python
import functools

import jax
import jax.numpy as jnp
from jax import lax
from jax.experimental import pallas as pl
from jax.experimental.pallas import tpu as pltpu

NORM_EPS = 1e-6
GLA_HEADS = 4
GLA_GATE_RANK = 16
GLA_GATE_NORMALIZER = 16.0
GLA_CHUNK = 64
GLA_SUB = 16
MOBA_HEADS = 16
MOBA_BLOCK = 256
MOBA_TOP_K = 3
ROPE_THETA = 500000.0
ROPE_FRACTION = 4
PAGE_SIZE = 128

V7X_LANES = 128
V7X_VMEM_LIMIT_BYTES = 56 * 1024 * 1024

F32 = jnp.float32
BF16 = jnp.bfloat16
NEG_INF = float("-inf")


def _params(*sem):
    return pltpu.CompilerParams(dimension_semantics=sem, vmem_limit_bytes=V7X_VMEM_LIMIT_BYTES)


def _pick(n, candidates):
    for c in candidates:
        if n % c == 0:
            return c
    raise ValueError(f"no tile for {n} in {candidates}")


def _dot(a, b):
    return jnp.dot(a, b, preferred_element_type=F32)


def _dot_nt(a, b):
    return lax.dot_general(a, b, (((1,), (1,)), ((), ())), preferred_element_type=F32)


def _dot_tn(a, b):
    return lax.dot_general(a, b, (((0,), (0,)), ((), ())), preferred_element_type=F32)


def _split2(x):
    hi = x.astype(BF16)
    lo = (x - hi.astype(F32)).astype(BF16)
    return hi, lo


def _split3(x):
    hi = x.astype(BF16)
    r1 = x - hi.astype(F32)
    mid = r1.astype(BF16)
    lo = (r1 - mid.astype(F32)).astype(BF16)
    return hi, mid, lo


def _rmsnorm_kernel(x_ref, g_ref, o_ref):
    x = x_ref[...]
    ms = jnp.mean(x * x, axis=-1, keepdims=True)
    o_ref[...] = (x * lax.rsqrt(ms + NORM_EPS) * g_ref[...]).astype(o_ref.dtype)


def _rmsnorm(x, g, out_dtype):
    m, d = x.shape
    tm = _pick(m, (512, 256, 128, 64, 32, 16))
    return pl.pallas_call(
        _rmsnorm_kernel,
        grid=(m // tm,),
        in_specs=[pl.BlockSpec((tm, d), lambda i: (i, 0)),
                  pl.BlockSpec((1, d), lambda i: (0, 0))],
        out_specs=pl.BlockSpec((tm, d), lambda i: (i, 0)),
        out_shape=jax.ShapeDtypeStruct((m, d), out_dtype),
        compiler_params=_params("parallel"),
        name="rmsnorm",
    )(x, g.reshape(1, d))


def _cache_weight(w_ref, wb_ref):
    @pl.when(pl.program_id(1) == 0)
    def _():
        wb_ref[...] = w_ref[...].astype(BF16)


def _mm_kernel(x_ref, w_ref, o_ref, wb_ref):
    _cache_weight(w_ref, wb_ref)
    o_ref[...] = _dot(x_ref[...].astype(BF16), wb_ref[...]).astype(o_ref.dtype)


def _mm_res_kernel(x_ref, w_ref, r_ref, o_ref, wb_ref):
    _cache_weight(w_ref, wb_ref)
    o_ref[...] = r_ref[...] + _dot(x_ref[...].astype(BF16), wb_ref[...])


def _mm_rope_kernel(x_ref, w_ref, cos_ref, sin_ref, o_ref, wb_ref, *, rot_half):
    _cache_weight(w_ref, wb_ref)
    acc = _dot(x_ref[...].astype(BF16), wb_ref[...])
    tn = acc.shape[1]
    reps = tn // V7X_LANES
    cos = jnp.concatenate([cos_ref[...]] * reps, axis=1)
    sin = jnp.concatenate([sin_ref[...]] * reps, axis=1)
    lane = lax.broadcasted_iota(jnp.int32, acc.shape, 1) % V7X_LANES
    partner = jnp.where(lane < rot_half,
                        pltpu.roll(acc, tn - rot_half, 1),
                        pltpu.roll(acc, rot_half, 1))
    o_ref[...] = (acc * cos + partner * sin).astype(o_ref.dtype)


def _mm_swiglu_kernel(x_ref, wg_ref, wu_ref, o_ref, wgb_ref, wub_ref):
    _cache_weight(wg_ref, wgb_ref)
    _cache_weight(wu_ref, wub_ref)
    x = x_ref[...]
    g = _dot(x, wgb_ref[...])
    u = _dot(x, wub_ref[...])
    o_ref[...] = (g * jax.nn.sigmoid(g) * u).astype(o_ref.dtype)


def _mm_tiles(m, k, n):
    tm = _pick(m, (1024, 512, 256, 128, 64, 32, 16))
    tn = _pick(n, (512, 256, 128))
    if k > 4096:
        tm = min(tm, 512)
    return tm, tn


def _matmul(x, w, layer, n, col0=0, out_dtype=F32, res=None, rope=None):
    m, k = x.shape
    tm, tn = _mm_tiles(m, k, n)
    assert col0 % tn == 0 and w.shape[1] == k
    cb0 = col0 // tn
    x_spec = pl.BlockSpec((tm, k), lambda j, i: (i, 0))
    w_spec = pl.BlockSpec((None, k, tn), lambda j, i: (layer, 0, cb0 + j))
    o_spec = pl.BlockSpec((tm, tn), lambda j, i: (i, j))
    scratch = [pltpu.VMEM((k, tn), BF16)]
    if res is not None:
        body, ins, specs = _mm_res_kernel, (x, w, res), [x_spec, w_spec, o_spec]
    elif rope is not None:
        cos, sin, rot_half = rope
        assert cos.shape[0] % tm == 0
        period = cos.shape[0] // tm
        t_spec = pl.BlockSpec((tm, V7X_LANES), lambda j, i: (i % period, 0))
        body = functools.partial(_mm_rope_kernel, rot_half=rot_half)
        ins, specs = (x, w, cos, sin), [x_spec, w_spec, t_spec, t_spec]
    else:
        body, ins, specs = _mm_kernel, (x, w), [x_spec, w_spec]
    return pl.pallas_call(
        body,
        grid=(n // tn, m // tm),
        in_specs=specs,
        out_specs=o_spec,
        out_shape=jax.ShapeDtypeStruct((m, n), out_dtype),
        scratch_shapes=scratch,
        compiler_params=_params("arbitrary", "arbitrary"),
        name="matmul",
    )(*ins)


def _swiglu_up(x, w_gate_up, layer):
    m, k = x.shape
    f = w_gate_up.shape[2] // 2
    tm, tn = _mm_tiles(m, k, f)
    nb = f // tn
    return pl.pallas_call(
        _mm_swiglu_kernel,
        grid=(nb, m // tm),
        in_specs=[pl.BlockSpec((tm, k), lambda j, i: (i, 0)),
                  pl.BlockSpec((None, k, tn), lambda j, i: (layer, 0, j)),
                  pl.BlockSpec((None, k, tn), lambda j, i: (layer, 0, nb + j))],
        out_specs=pl.BlockSpec((tm, tn), lambda j, i: (i, j)),
        out_shape=jax.ShapeDtypeStruct((m, f), BF16),
        scratch_shapes=[pltpu.VMEM((k, tn), BF16), pltpu.VMEM((k, tn), BF16)],
        compiler_params=_params("arbitrary", "arbitrary"),
        name="swiglu_up",
    )(x, w_gate_up, w_gate_up)


def _ffn(x, g, w_gate_up, w_down, layer):
    h = _rmsnorm(x, g, BF16)
    hid = _swiglu_up(h, w_gate_up, layer)
    return _matmul(hid, w_down, layer, w_down.shape[2], res=x)


def _rope_table_kernel(inv_ref, cos_ref, sin_ref, *, base, period, rot_half):
    rows = cos_ref.shape[0]
    row = lax.broadcasted_iota(jnp.int32, (rows, V7X_LANES), 0) + pl.program_id(0) * rows
    lane = lax.broadcasted_iota(jnp.int32, (rows, V7X_LANES), 1)
    pos = (base + (row & (period - 1))).astype(F32)
    ang = pos * inv_ref[...]
    c, s = jnp.cos(ang), jnp.sin(ang)
    cos_ref[...] = jnp.where(lane < 2 * rot_half, c, 1.0)
    sin_ref[...] = jnp.where(lane < rot_half, -s, jnp.where(lane < 2 * rot_half, s, 0.0))


def _rope_tables(rows, base, period, head_dim):
    assert period & (period - 1) == 0 and head_dim == V7X_LANES
    rot_half = head_dim // ROPE_FRACTION // 2
    inv = ROPE_THETA ** (-jnp.arange(rot_half, dtype=F32) / rot_half)
    inv = jnp.concatenate([inv, inv, jnp.zeros((head_dim - 2 * rot_half,), F32)]).reshape(1, head_dim)
    tr = _pick(rows, (256, 128, 64, 32, 16, 8))
    spec = pl.BlockSpec((tr, head_dim), lambda i: (i, 0))
    cos, sin = pl.pallas_call(
        functools.partial(_rope_table_kernel, base=base, period=period, rot_half=rot_half),
        grid=(rows // tr,),
        in_specs=[pl.BlockSpec((1, head_dim), lambda i: (0, 0))],
        out_specs=[spec, spec],
        out_shape=[jax.ShapeDtypeStruct((rows, head_dim), F32)] * 2,
        compiler_params=_params("parallel"),
        name="rope_tables",
    )(inv)
    return cos, sin, rot_half


def _gla_gate_kernel(h_ref, wlr_ref, wg_ref, b_ref, o_ref):
    rank = wg_ref.shape[0]
    glr = _dot(h_ref[...], wlr_ref[:, :rank].astype(BF16))
    z = _dot(glr.astype(BF16), wg_ref[...].astype(BF16)) + b_ref[...]
    o_ref[...] = (jnp.minimum(z, 0.0) - jnp.log1p(jnp.exp(-jnp.abs(z)))) * (1.0 / GLA_GATE_NORMALIZER)


def _gla_gate(h, w_in, layer, col0, w_gate, b_gate):
    m, d = h.shape
    r, kd = w_gate.shape
    assert col0 % V7X_LANES == 0 and col0 + r == w_in.shape[2] and r <= V7X_LANES
    tm = _pick(m, (512, 256, 128, 64, 32, 16))
    return pl.pallas_call(
        _gla_gate_kernel,
        grid=(m // tm,),
        in_specs=[pl.BlockSpec((tm, d), lambda i: (i, 0)),
                  pl.BlockSpec((None, d, V7X_LANES), lambda i: (layer, 0, col0 // V7X_LANES)),
                  pl.BlockSpec((r, kd), lambda i: (0, 0)),
                  pl.BlockSpec((1, kd), lambda i: (0, 0))],
        out_specs=pl.BlockSpec((tm, kd), lambda i: (i, 0)),
        out_shape=jax.ShapeDtypeStruct((m, kd), F32),
        compiler_params=_params("parallel"),
        name="gla_gate",
    )(h, w_in, w_gate, b_gate.reshape(1, kd))


def _gla_kernel(q_ref, k_ref, v_ref, g_ref, r_ref, hn_ref, s0_ref, o_ref, s_ref, st_ref, *, sub, n_valid):
    ci = pl.program_id(2)
    chunk, dk = q_ref.shape

    @pl.when(ci == 0)
    def _():
        st_ref[...] = s0_ref[0, 0].T

    q = q_ref[...] * (dk ** -0.5)
    k = k_ref[...]
    v = v_ref[...].astype(BF16)
    tok = lax.broadcasted_iota(jnp.int32, (chunk, 1), 0) + ci * chunk
    g = jnp.where(tok < n_valid, g_ref[...], 0.0)

    row = lax.broadcasted_iota(jnp.int32, (chunk, chunk), 0)
    col = lax.broadcasted_iota(jnp.int32, (chunk, chunk), 1)
    tri = jnp.where(col <= row, 1.0, 0.0).astype(BF16)
    g_hi, g_mid, g_lo = _split3(g)
    b = _dot(tri, g_hi) + _dot(tri, g_mid) + _dot(tri, g_lo)
    b_last = b[chunk - 1:chunk, :]

    st = st_ref[...]
    o = _dot_nt((q * jnp.exp(b)).astype(BF16), st.astype(BF16))

    krow = lax.broadcasted_iota(jnp.int32, (chunk, 1), 0)
    srow = lax.broadcasted_iota(jnp.int32, (sub, 1), 0)
    scol = lax.broadcasted_iota(jnp.int32, (sub, chunk), 1)
    blocks = []
    for i in range(chunk // sub):
        lo = i * sub
        bi, qi, ki = b[lo:lo + sub], q[lo:lo + sub], k[lo:lo + sub]
        a = jnp.zeros((sub, chunk), F32)
        for jj in range(sub):
            decay = jnp.exp(jnp.where(srow >= jj, bi - bi[jj:jj + 1], NEG_INF))
            pair = jnp.sum(qi * decay * ki[jj:jj + 1], axis=-1, keepdims=True)
            a = jnp.where(scol == lo + jj, pair, a)
        if i > 0:
            ref_b = bi[0:1]
            qs = (qi * jnp.exp(bi - ref_b)).astype(BF16)
            ks = (k * jnp.exp(jnp.where(krow < lo, ref_b - b, NEG_INF))).astype(BF16)
            a = a + _dot_nt(qs, ks)
        blocks.append(a)
    attn = jnp.concatenate(blocks, axis=0)
    o = o + _dot(attn.astype(BF16), v)

    k_dec = (k * jnp.exp(b_last - b)).astype(BF16)
    st_new = st * jnp.exp(b_last) + _dot_tn(v, k_dec)
    st_ref[...] = st_new

    @pl.when(ci == pl.num_programs(2) - 1)
    def _():
        s_ref[0, 0] = st_new.T

    ms = jnp.mean(o * o, axis=-1, keepdims=True)
    y = o * lax.rsqrt(ms + NORM_EPS) * hn_ref[...]
    r = r_ref[...]
    o_ref[...] = (y * (r * jax.nn.sigmoid(r))).astype(o_ref.dtype)


def _gla_core(proj, g, head_norm, s0, seq, n_valid):
    m = proj.shape[0]
    bsz, heads, dk, dv = s0.shape
    chunk = GLA_CHUNK
    assert seq % chunk == 0 and chunk % GLA_SUB == 0 and (2 * heads * dk) % dv == 0
    nc = seq // chunk
    kb0 = heads
    vb0 = 2 * heads * dk // dv
    rb0 = vb0 + heads

    def rows(bi, hi, ci):
        return bi * nc + ci

    return pl.pallas_call(
        functools.partial(_gla_kernel, sub=GLA_SUB, n_valid=n_valid),
        grid=(bsz, heads, nc),
        in_specs=[pl.BlockSpec((chunk, dk), lambda bi, hi, ci: (rows(bi, hi, ci), hi)),
                  pl.BlockSpec((chunk, dk), lambda bi, hi, ci: (rows(bi, hi, ci), kb0 + hi)),
                  pl.BlockSpec((chunk, dv), lambda bi, hi, ci: (rows(bi, hi, ci), vb0 + hi)),
                  pl.BlockSpec((chunk, dk), lambda bi, hi, ci: (rows(bi, hi, ci), hi)),
                  pl.BlockSpec((chunk, dv), lambda bi, hi, ci: (rows(bi, hi, ci), rb0 + hi)),
                  pl.BlockSpec((1, dv), lambda bi, hi, ci: (0, 0)),
                  pl.BlockSpec((1, 1, dk, dv), lambda bi, hi, ci: (bi, hi, 0, 0))],
        out_specs=[pl.BlockSpec((chunk, dv), lambda bi, hi, ci: (rows(bi, hi, ci), hi)),
                   pl.BlockSpec((1, 1, dk, dv), lambda bi, hi, ci: (bi, hi, 0, 0))],
        out_shape=[jax.ShapeDtypeStruct((m, heads * dv), BF16),
                   jax.ShapeDtypeStruct((bsz, heads, dk, dv), F32)],
        scratch_shapes=[pltpu.VMEM((dv, dk), F32)],
        compiler_params=_params("arbitrary", "arbitrary", "arbitrary"),
        name="gla_core",
    )(proj, proj, proj, g, proj, head_norm.reshape(1, dv), s0)


def _gla_mixer(x, h, s0, seq, n_valid, layer, w_in, w_gate, b_gate, head_norm, w_out):
    heads, dk, dv = s0.shape[1:]
    n_main = 2 * heads * dk + 2 * heads * dv
    proj = _matmul(h, w_in, layer, n_main)
    g = _gla_gate(h, w_in, layer, n_main, w_gate, b_gate)
    o, s_fin = _gla_core(proj, g, head_norm, s0, seq, n_valid)
    return _matmul(o, w_out, layer, w_out.shape[2], res=x), s_fin


def _moba_prompt_kernel(q_ref, k_ref, v_ref, o_ref, km_ref, *, top_k):
    i = pl.program_id(2)
    blk, hd = q_ref.shape
    nb = k_ref.shape[0] // blk
    scale = hd ** -0.5

    @pl.when(i == 0)
    def _():
        for n in range(nb):
            km_ref[n:n + 1, :] = jnp.mean(k_ref[n * blk:(n + 1) * blk, :], axis=0, keepdims=True)

    q = q_ref[...]
    q_hi, q_lo = _split2(q)
    km_hi, km_lo = _split2(km_ref[...])
    gate = _dot_nt(q_hi, km_hi) + _dot_nt(q_hi, km_lo) + _dot_nt(q_lo, km_hi)

    bcol = lax.broadcasted_iota(jnp.int32, (blk, nb), 1)
    past = bcol < i
    gate = jnp.where(past, gate, NEG_INF)
    rank = jnp.zeros((blk, nb), jnp.int32)
    for n in range(nb):
        gn = gate[:, n:n + 1]
        ahead = jnp.where(gn > gate, 1, jnp.where(gn == gate, jnp.where(bcol > n, 1, 0), 0))
        rank = rank + ahead
    sel = jnp.where(past, jnp.where(rank < top_k, 1.0, 0.0), 0.0)

    qb = q.astype(BF16)
    row = lax.broadcasted_iota(jnp.int32, (blk, blk), 0)
    col = lax.broadcasted_iota(jnp.int32, (blk, blk), 1)
    own = pl.multiple_of(i * blk, blk)
    s = _dot_nt(qb, k_ref[pl.ds(own, blk), :].astype(BF16)) * scale
    s = jnp.where(col <= row, s, NEG_INF)
    m0 = jnp.max(s, axis=-1, keepdims=True)
    p = jnp.exp(s - m0)
    l0 = jnp.sum(p, axis=-1, keepdims=True)
    acc0 = _dot(p.astype(BF16), v_ref[pl.ds(own, blk), :].astype(BF16))

    def past_block(n, carry):
        m_run, l_run, acc = carry
        start = pl.multiple_of(n * blk, blk)
        s = _dot_nt(qb, k_ref[pl.ds(start, blk), :].astype(BF16)) * scale
        picked = jnp.sum(jnp.where(bcol == n, sel, 0.0), axis=-1, keepdims=True) > 0.0
        s = jnp.where(picked, s, NEG_INF)
        m_new = jnp.maximum(m_run, jnp.max(s, axis=-1, keepdims=True))
        alpha = jnp.exp(m_run - m_new)
        p = jnp.exp(s - m_new)
        l_new = alpha * l_run + jnp.sum(p, axis=-1, keepdims=True)
        acc = alpha * acc + _dot(p.astype(BF16), v_ref[pl.ds(start, blk), :].astype(BF16))
        return m_new, l_new, acc

    _, l_fin, acc = lax.fori_loop(0, i, past_block, (m0, l0, acc0))
    o_ref[...] = (acc / l_fin).astype(o_ref.dtype)


def _moba_prompt(q, k, v, bsz, seq, heads):
    m, d = q.shape
    hd = d // heads
    blk = MOBA_BLOCK
    assert seq % blk == 0 and hd == V7X_LANES
    nb = seq // blk
    kv_spec = pl.BlockSpec((seq, hd), lambda bi, hi, i: (bi, hi))
    qo_spec = pl.BlockSpec((blk, hd), lambda bi, hi, i: (bi * nb + i, hi))
    return pl.pallas_call(
        functools.partial(_moba_prompt_kernel, top_k=MOBA_TOP_K),
        grid=(bsz, heads, nb),
        in_specs=[qo_spec, kv_spec, kv_spec],
        out_specs=qo_spec,
        out_shape=jax.ShapeDtypeStruct((m, d), BF16),
        scratch_shapes=[pltpu.VMEM((nb, hd), F32)],
        compiler_params=_params("arbitrary", "arbitrary", "arbitrary"),
        name="moba_prompt",
    )(q, k, v)


def _kmean_kernel(pt_ref, c0_ref, c1_ref, o_ref):
    n = pl.program_id(1)
    rows = c0_ref.shape[2] + c1_ref.shape[2]
    total = jnp.sum(c0_ref[0, 0], axis=0) + jnp.sum(c1_ref[0, 0], axis=0)
    o_ref[0, n] = total * (1.0 / rows)


def _past_block_means(cache, page_table, layer):
    bsz, n_pages = page_table.shape
    _, _, page, heads, hd = cache.shape
    assert MOBA_BLOCK == 2 * page
    nbp = n_pages // 2
    grid_spec = pltpu.PrefetchScalarGridSpec(
        num_scalar_prefetch=1,
        grid=(bsz, nbp),
        in_specs=[pl.BlockSpec((1, 1, page, heads, hd), lambda bi, n, pt: (layer, pt[bi, 2 * n], 0, 0, 0)),
                  pl.BlockSpec((1, 1, page, heads, hd), lambda bi, n, pt: (layer, pt[bi, 2 * n + 1], 0, 0, 0))],
        out_specs=pl.BlockSpec((1, nbp, heads, hd), lambda bi, n, pt: (bi, 0, 0, 0)),
    )
    return pl.pallas_call(
        _kmean_kernel,
        grid_spec=grid_spec,
        out_shape=jax.ShapeDtypeStruct((bsz, nbp, heads, hd), F32),
        compiler_params=_params("arbitrary", "arbitrary"),
        name="moba_kmean",
    )(page_table, cache, cache)


def _moba_select_kernel(q_ref, km_ref, o_ref, *, top_k, n_q):
    km = km_ref[0]
    nbp, heads, _ = km.shape
    bidx = lax.broadcasted_iota(jnp.int32, (nbp, heads, 1), 0)
    for t in range(n_q):
        gate = jnp.sum(km * q_ref[0, t][None], axis=-1, keepdims=True)
        for r in range(top_k):
            best = jnp.max(gate, axis=0, keepdims=True)
            idx = jnp.min(jnp.where(gate == best, bidx, nbp), axis=0, keepdims=True)
            o_ref[0, t * top_k + r] = jnp.broadcast_to(idx[0], o_ref.shape[2:])
            gate = jnp.where(bidx == idx, NEG_INF, gate)


def _moba_select(q, kmean, n_q):
    bsz, _, heads, hd = q.shape
    nbp = kmean.shape[1]
    slots = n_q * MOBA_TOP_K
    sel = pl.pallas_call(
        functools.partial(_moba_select_kernel, top_k=MOBA_TOP_K, n_q=n_q),
        grid=(bsz,),
        in_specs=[pl.BlockSpec((1, n_q, heads, hd), lambda bi: (bi, 0, 0, 0)),
                  pl.BlockSpec((1, nbp, heads, hd), lambda bi: (bi, 0, 0, 0))],
        out_specs=pl.BlockSpec((1, slots, heads, V7X_LANES), lambda bi: (bi, 0, 0, 0)),
        out_shape=jax.ShapeDtypeStruct((bsz, slots, heads, V7X_LANES), jnp.int32),
        compiler_params=_params("parallel"),
        name="moba_select",
    )(q, kmean)
    return sel[..., 0]


def _moba_decode_kernel(pt_ref, sel_ref, q_ref, kn_ref, vn_ref, kc_hbm, vc_hbm, o_ref,
                        kbuf, vbuf, sems, *, layer, top_k, pages_per_block, n_q):
    b = pl.program_id(0)
    h = pl.program_id(1)
    heads = pl.num_programs(1)
    n_slots = n_q * top_k
    n_pages = n_slots * pages_per_block
    hd = q_ref.shape[2]
    scale = hd ** -0.5

    def page_copies(i):
        slot, pg = divmod(i, pages_per_block)
        blk = sel_ref[(b * n_slots + slot) * heads + h]
        phys = pt_ref[b, blk * pages_per_block + pg]
        return (pltpu.make_async_copy(kc_hbm.at[layer, phys, :, h, :], kbuf.at[i], sems.at[0, i]),
                pltpu.make_async_copy(vc_hbm.at[layer, phys, :, h, :], vbuf.at[i], sems.at[1, i]))

    for i in range(n_pages):
        for c in page_copies(i):
            c.start()
    for i in range(n_pages):
        for c in page_copies(i):
            c.wait()

    rows = kn_ref.shape[1]
    krow = lax.broadcasted_iota(jnp.int32, (rows, 1), 0)
    o_ref[...] = jnp.zeros(o_ref.shape, o_ref.dtype)
    for t in range(n_q):
        qt = q_ref[0, t:t + 1, :]
        s_own = jnp.sum(kn_ref[0] * qt, axis=-1, keepdims=True) * scale
        scores = [jnp.where(krow <= t, s_own, NEG_INF)]
        values = [vn_ref[0]]
        for i in range(t * top_k * pages_per_block, (t + 1) * top_k * pages_per_block):
            scores.append(jnp.sum(kbuf[i] * qt, axis=-1, keepdims=True) * scale)
            values.append(vbuf[i])
        m = functools.reduce(jnp.maximum, [jnp.max(s, axis=0, keepdims=True) for s in scores])
        l = jnp.zeros((1, 1), F32)
        acc = jnp.zeros((1, hd), F32)
        for s, val in zip(scores, values):
            p = jnp.exp(s - m)
            l = l + jnp.sum(p, axis=0, keepdims=True)
            acc = acc + jnp.sum(p * val, axis=0, keepdims=True)
        o_ref[0, t:t + 1, :] = acc / l


def _moba_decode(q, k_new, v_new, cache_k, cache_v, page_table, sel, layer, n_q):
    bsz, rows, d = q.shape
    _, _, page, heads, hd = cache_k.shape
    ppb = MOBA_BLOCK // page
    top_k = MOBA_TOP_K
    n_pages = n_q * top_k * ppb
    row_spec = pl.BlockSpec((1, rows, hd), lambda bi, hi, pt, sl: (bi, 0, hi))
    any_spec = pl.BlockSpec(memory_space=pl.ANY)
    grid_spec = pltpu.PrefetchScalarGridSpec(
        num_scalar_prefetch=2,
        grid=(bsz, heads),
        in_specs=[row_spec, row_spec, row_spec, any_spec, any_spec],
        out_specs=row_spec,
        scratch_shapes=[pltpu.VMEM((n_pages, page, hd), F32), pltpu.VMEM((n_pages, page, hd), F32),
                        pltpu.SemaphoreType.DMA((2, n_pages))],
    )
    return pl.pallas_call(
        functools.partial(_moba_decode_kernel, layer=layer, top_k=top_k, pages_per_block=ppb, n_q=n_q),
        grid_spec=grid_spec,
        out_shape=jax.ShapeDtypeStruct((bsz, rows, d), F32),
        compiler_params=_params("arbitrary", "arbitrary"),
        name="moba_decode",
    )(page_table, sel.reshape(-1), q, k_new, v_new, cache_k, cache_v)


def kernel(x_prompt, x_sample, state_gla, cache_k, cache_v, page_table, norm_mix, norm_ffn, norm_final,
           gla_w_in, gla_w_gate, gla_b_gate, gla_head_norm, gla_w_out, moba_w_qkv, moba_w_o,
           ffn_w_gate_up, ffn_w_down):
    bp, seq, d = x_prompt.shape
    bd, sd, _ = x_sample.shape
    depth = norm_mix.shape[0]
    n_moba, n_pool, page, mh, mhd = cache_k.shape
    past_len = page_table.shape[1] * page
    assert past_len % MOBA_BLOCK == 0 and sd <= min(MOBA_BLOCK, 8) and GLA_CHUNK % sd == 0
    gh, gdk, gdv = state_gla.shape[2:]

    xp = x_prompt.reshape(bp * seq, d)
    xs = x_sample.reshape(bd * sd, d)
    rope_p = _rope_tables(seq, 0, seq, mhd)
    rope_s = _rope_tables(bd * sd, past_len, sd, mhd)
    zero_state = jnp.zeros((bp, gh, gdk, gdv), F32)
    pad_s = GLA_CHUNK - sd

    def pad_tokens(a, n):
        c = a.shape[1]
        return jnp.pad(a.reshape(bd, sd, c), ((0, 0), (0, n), (0, 0))).reshape(bd * (sd + n), c)

    gla_sp, gla_ss, kp_l, vp_l, ks_l, vs_l = [], [], [], [], [], []
    for i in range(depth):
        j = i // 2
        hp = _rmsnorm(xp, norm_mix[i], BF16)
        hs = _rmsnorm(xs, norm_mix[i], BF16)
        if i % 2 == 0:
            args = (j, gla_w_in, gla_w_gate[j], gla_b_gate[j], gla_head_norm[j], gla_w_out)
            xp, sp = _gla_mixer(xp, hp, zero_state, seq, seq, *args)
            xs_pad, ss = _gla_mixer(pad_tokens(xs, pad_s), pad_tokens(hs, pad_s),
                                    state_gla[j], sd + pad_s, sd, *args)
            xs = xs_pad.reshape(bd, sd + pad_s, d)[:, :sd].reshape(bd * sd, d)
            gla_sp.append(sp)
            gla_ss.append(ss)
        else:
            w_qkv, w_o = moba_w_qkv, moba_w_o
            qp = _matmul(hp, w_qkv, j, d, 0, rope=rope_p)
            kp = _matmul(hp, w_qkv, j, d, d, rope=rope_p)
            vp = _matmul(hp, w_qkv, j, d, 2 * d)
            op = _moba_prompt(qp, kp, vp, bp, seq, mh)
            xp = _matmul(op, w_o, j, d, res=xp)
            kp_l.append(kp.reshape(bp * seq // page, page, mh, mhd))
            vp_l.append(vp.reshape(bp * seq // page, page, mh, mhd))

            qs = _matmul(hs, w_qkv, j, d, 0, rope=rope_s)
            ks = _matmul(hs, w_qkv, j, d, d, rope=rope_s)
            vs = _matmul(hs, w_qkv, j, d, 2 * d)
            rows = 8
            q3, k3, v3 = (pad_tokens(a, rows - sd).reshape(bd, rows, d) for a in (qs, ks, vs))
            kmean = _past_block_means(cache_k, page_table, j)
            sel = _moba_select(qs.reshape(bd, sd, mh, mhd), kmean, sd)
            o3 = _moba_decode(q3, k3, v3, cache_k, cache_v, page_table, sel, j, sd)
            xs = _matmul(o3[:, :sd].reshape(bd * sd, d), w_o, j, d, res=xs)
            ks_l.append(ks.reshape(bd, sd, mh, mhd))
            vs_l.append(vs.reshape(bd, sd, mh, mhd))
        xp = _ffn(xp, norm_ffn[i], ffn_w_gate_up, ffn_w_down, i)
        xs = _ffn(xs, norm_ffn[i], ffn_w_gate_up, ffn_w_down, i)

    y_prompt = _rmsnorm(xp, norm_final, F32).reshape(bp, seq, d)
    y_sample = _rmsnorm(xs, norm_final, F32).reshape(bd, sd, d)
    return (y_prompt, y_sample, jnp.stack(gla_sp), jnp.stack(gla_ss),
            jnp.stack(kp_l), jnp.stack(vp_l), jnp.stack(ks_l), jnp.stack(vs_l))
```

```python
import functools

import jax
import jax.numpy as jnp
from jax import lax
from jax.experimental import pallas as pl
from jax.experimental.pallas import tpu as pltpu

NORM_EPS = 1e-6
GLA_HEADS = 4
GLA_GATE_RANK = 16
GLA_GATE_NORMALIZER = 16.0
GLA_CHUNK = 64
GLA_SUB = 16
GLA_CHUNKS_PER_STEP = 4
MOBA_HEADS = 16
MOBA_BLOCK = 256
MOBA_TOP_K = 3
ROPE_THETA = 500000.0
ROPE_FRACTION = 4
PAGE_SIZE = 128

V7X_LANES = 128
V7X_VMEM_LIMIT_BYTES = 56 * 1024 * 1024

F32 = jnp.float32
BF16 = jnp.bfloat16
NEG_INF = float("-inf")
LOG2_E = 1.4426950408889634


def _params(*sem):
    return pltpu.CompilerParams(dimension_semantics=sem, vmem_limit_bytes=V7X_VMEM_LIMIT_BYTES)


def _pick(n, candidates):
    for c in candidates:
        if n % c == 0:
            return c
    raise ValueError(f"no tile for {n} in {candidates}")


def _dot(a, b):
    return jnp.dot(a, b, preferred_element_type=F32)


def _dot_nt(a, b):
    return lax.dot_general(a, b, (((1,), (1,)), ((), ())), preferred_element_type=F32)


def _dot_tn(a, b):
    return lax.dot_general(a, b, (((0,), (0,)), ((), ())), preferred_element_type=F32)


def _split2(x):
    hi = x.astype(BF16)
    lo = (x - hi.astype(F32)).astype(BF16)
    return hi, lo


def _split3(x):
    hi = x.astype(BF16)
    r1 = x - hi.astype(F32)
    mid = r1.astype(BF16)
    lo = (r1 - mid.astype(F32)).astype(BF16)
    return hi, mid, lo


def _rmsnorm_kernel(x_ref, g_ref, o_ref):
    x = x_ref[...]
    ms = jnp.mean(x * x, axis=-1, keepdims=True)
    o_ref[...] = (x * lax.rsqrt(ms + NORM_EPS) * g_ref[...]).astype(o_ref.dtype)


def _rmsnorm(x, g, out_dtype):
    m, d = x.shape
    tm = _pick(m, (512, 256, 128, 64, 32, 16))
    return pl.pallas_call(
        _rmsnorm_kernel,
        grid=(m // tm,),
        in_specs=[pl.BlockSpec((tm, d), lambda i: (i, 0)),
                  pl.BlockSpec((1, d), lambda i: (0, 0))],
        out_specs=pl.BlockSpec((tm, d), lambda i: (i, 0)),
        out_shape=jax.ShapeDtypeStruct((m, d), out_dtype),
        compiler_params=_params("parallel"),
        name="rmsnorm",
    )(x, g.reshape(1, d))


def _cache_weight(w_ref, wb_ref):
    @pl.when(pl.program_id(1) == 0)
    def _():
        wb_ref[...] = w_ref[...].astype(BF16)


def _mm_kernel(x_ref, w_ref, o_ref, wb_ref):
    _cache_weight(w_ref, wb_ref)
    o_ref[...] = _dot(x_ref[...].astype(BF16), wb_ref[...]).astype(o_ref.dtype)


def _mm_res_kernel(x_ref, w_ref, r_ref, o_ref, wb_ref):
    _cache_weight(w_ref, wb_ref)
    o_ref[...] = r_ref[...] + _dot(x_ref[...].astype(BF16), wb_ref[...])


def _mm_rope_kernel(x_ref, w_ref, cos_ref, sin_ref, o_ref, wb_ref, *, rot_half):
    _cache_weight(w_ref, wb_ref)
    acc = _dot(x_ref[...].astype(BF16), wb_ref[...])
    tn = acc.shape[1]
    reps = tn // V7X_LANES
    cos = jnp.concatenate([cos_ref[...]] * reps, axis=1)
    sin = jnp.concatenate([sin_ref[...]] * reps, axis=1)
    lane = lax.broadcasted_iota(jnp.int32, acc.shape, 1) % V7X_LANES
    partner = jnp.where(lane < rot_half,
                        pltpu.roll(acc, tn - rot_half, 1),
                        pltpu.roll(acc, rot_half, 1))
    o_ref[...] = (acc * cos + partner * sin).astype(o_ref.dtype)


def _mm_swiglu_kernel(x_ref, wg_ref, wu_ref, o_ref, wgb_ref, wub_ref):
    _cache_weight(wg_ref, wgb_ref)
    _cache_weight(wu_ref, wub_ref)
    x = x_ref[...]
    g = _dot(x, wgb_ref[...])
    u = _dot(x, wub_ref[...])
    o_ref[...] = (g * jax.nn.sigmoid(g) * u).astype(o_ref.dtype)


def _mm_tiles(m, k, n):
    tm = _pick(m, (1024, 512, 256, 128, 64, 32, 16))
    tn = _pick(n, (1024, 512, 256, 128))
    if k > 4096:
        tm, tn = min(tm, 512), min(tn, 512)
    return tm, tn


def _matmul(x, w, layer, n, col0=0, out_dtype=F32, res=None, rope=None):
    m, k = x.shape
    tm, tn = _mm_tiles(m, k, n)
    assert col0 % tn == 0 and w.shape[1] == k
    cb0 = col0 // tn
    x_spec = pl.BlockSpec((tm, k), lambda j, i: (i, 0))
    w_spec = pl.BlockSpec((None, k, tn), lambda j, i: (layer, 0, cb0 + j))
    o_spec = pl.BlockSpec((tm, tn), lambda j, i: (i, j))
    scratch = [pltpu.VMEM((k, tn), BF16)]
    if res is not None:
        body, ins, specs = _mm_res_kernel, (x, w, res), [x_spec, w_spec, o_spec]
    elif rope is not None:
        cos, sin, rot_half = rope
        assert cos.shape[0] % tm == 0
        period = cos.shape[0] // tm
        t_spec = pl.BlockSpec((tm, V7X_LANES), lambda j, i: (i % period, 0))
        body = functools.partial(_mm_rope_kernel, rot_half=rot_half)
        ins, specs = (x, w, cos, sin), [x_spec, w_spec, t_spec, t_spec]
    else:
        body, ins, specs = _mm_kernel, (x, w), [x_spec, w_spec]
    return pl.pallas_call(
        body,
        grid=(n // tn, m // tm),
        in_specs=specs,
        out_specs=o_spec,
        out_shape=jax.ShapeDtypeStruct((m, n), out_dtype),
        scratch_shapes=scratch,
        compiler_params=_params("arbitrary", "arbitrary"),
        name="matmul",
    )(*ins)


def _swiglu_up(x, w_gate_up, layer):
    m, k = x.shape
    f = w_gate_up.shape[2] // 2
    tm, tn = _mm_tiles(m, k, f)
    nb = f // tn
    return pl.pallas_call(
        _mm_swiglu_kernel,
        grid=(nb, m // tm),
        in_specs=[pl.BlockSpec((tm, k), lambda j, i: (i, 0)),
                  pl.BlockSpec((None, k, tn), lambda j, i: (layer, 0, j)),
                  pl.BlockSpec((None, k, tn), lambda j, i: (layer, 0, nb + j))],
        out_specs=pl.BlockSpec((tm, tn), lambda j, i: (i, j)),
        out_shape=jax.ShapeDtypeStruct((m, f), BF16),
        scratch_shapes=[pltpu.VMEM((k, tn), BF16), pltpu.VMEM((k, tn), BF16)],
        compiler_params=_params("arbitrary", "arbitrary"),
        name="swiglu_up",
    )(x, w_gate_up, w_gate_up)


def _ffn(x, g, w_gate_up, w_down, layer):
    h = _rmsnorm(x, g, BF16)
    hid = _swiglu_up(h, w_gate_up, layer)
    return _matmul(hid, w_down, layer, w_down.shape[2], res=x)


def _rope_table_kernel(inv_ref, cos_ref, sin_ref, *, base, period, rot_half):
    rows = cos_ref.shape[0]
    row = lax.broadcasted_iota(jnp.int32, (rows, V7X_LANES), 0) + pl.program_id(0) * rows
    lane = lax.broadcasted_iota(jnp.int32, (rows, V7X_LANES), 1)
    pos = (base + (row & (period - 1))).astype(F32)
    ang = pos * inv_ref[...]
    c, s = jnp.cos(ang), jnp.sin(ang)
    cos_ref[...] = jnp.where(lane < 2 * rot_half, c, 1.0)
    sin_ref[...] = jnp.where(lane < rot_half, -s, jnp.where(lane < 2 * rot_half, s, 0.0))


def _rope_tables(rows, base, period, head_dim):
    assert period & (period - 1) == 0 and head_dim == V7X_LANES
    rot_half = head_dim // ROPE_FRACTION // 2
    inv = ROPE_THETA ** (-jnp.arange(rot_half, dtype=F32) / rot_half)
    inv = jnp.concatenate([inv, inv, jnp.zeros((head_dim - 2 * rot_half,), F32)]).reshape(1, head_dim)
    tr = _pick(rows, (256, 128, 64, 32, 16, 8))
    spec = pl.BlockSpec((tr, head_dim), lambda i: (i, 0))
    cos, sin = pl.pallas_call(
        functools.partial(_rope_table_kernel, base=base, period=period, rot_half=rot_half),
        grid=(rows // tr,),
        in_specs=[pl.BlockSpec((1, head_dim), lambda i: (0, 0))],
        out_specs=[spec, spec],
        out_shape=[jax.ShapeDtypeStruct((rows, head_dim), F32)] * 2,
        compiler_params=_params("parallel"),
        name="rope_tables",
    )(inv)
    return cos, sin, rot_half


def _gla_gate_kernel(h_ref, wlr_ref, wg_ref, b_ref, o_ref):
    rank = wg_ref.shape[0]
    glr = _dot(h_ref[...], wlr_ref[:, :rank].astype(BF16))
    z = _dot(glr.astype(BF16), wg_ref[...].astype(BF16)) + b_ref[...]
    o_ref[...] = (jnp.minimum(z, 0.0) - jnp.log1p(jnp.exp(-jnp.abs(z)))) * (1.0 / GLA_GATE_NORMALIZER)


def _gla_gate(h, w_in, layer, col0, w_gate, b_gate):
    m, d = h.shape
    r, kd = w_gate.shape
    assert col0 % V7X_LANES == 0 and col0 + r == w_in.shape[2] and r <= V7X_LANES
    tm = _pick(m, (512, 256, 128, 64, 32, 16))
    return pl.pallas_call(
        _gla_gate_kernel,
        grid=(m // tm,),
        in_specs=[pl.BlockSpec((tm, d), lambda i: (i, 0)),
                  pl.BlockSpec((None, d, V7X_LANES), lambda i: (layer, 0, col0 // V7X_LANES)),
                  pl.BlockSpec((r, kd), lambda i: (0, 0)),
                  pl.BlockSpec((1, kd), lambda i: (0, 0))],
        out_specs=pl.BlockSpec((tm, kd), lambda i: (i, 0)),
        out_shape=jax.ShapeDtypeStruct((m, kd), F32),
        compiler_params=_params("parallel"),
        name="gla_gate",
    )(h, w_in, w_gate, b_gate.reshape(1, kd))


def _gla_kernel(q_ref, k_ref, v_ref, g_ref, r_ref, hn_ref, s0_ref, o_ref, s_ref, st_ref, *,
                chunk, sub, n_valid):
    ci = pl.program_id(2)
    span, dk = q_ref.shape
    half = sub // 2

    @pl.when(ci == 0)
    def _():
        st_ref[...] = s0_ref[0, 0].T

    row = lax.broadcasted_iota(jnp.int32, (chunk, chunk), 0)
    col = lax.broadcasted_iota(jnp.int32, (chunk, chunk), 1)
    tri = jnp.where(col <= row, 1.0, 0.0).astype(BF16)
    krow = lax.broadcasted_iota(jnp.int32, (chunk, 1), 0)
    hrow = lax.broadcasted_iota(jnp.int32, (half, 1), 0)
    hcol = lax.broadcasted_iota(jnp.int32, (half, chunk), 1)
    hn = hn_ref[...]

    for c0 in range(0, span, chunk):
        q = q_ref[c0:c0 + chunk, :] * (dk ** -0.5)
        k = k_ref[c0:c0 + chunk, :]
        v = v_ref[c0:c0 + chunk, :].astype(BF16)
        tok = krow + (ci * span + c0)
        g = jnp.where(tok < n_valid, g_ref[c0:c0 + chunk, :], 0.0) * LOG2_E
        g_hi, g_mid, g_lo = _split3(g)
        b = _dot(tri, g_hi) + _dot(tri, g_mid) + _dot(tri, g_lo)
        b_last = b[chunk - 1:chunk, :]

        st = st_ref[...]
        o = _dot_nt((q * jnp.exp2(b)).astype(BF16), st.astype(BF16))

        blocks = []
        for i in range(chunk // sub):
            lo = i * sub
            bi, qi, ki = b[lo:lo + sub], q[lo:lo + sub], k[lo:lo + sub]
            for hh in range(2):
                bh, qh = bi[hh * half:(hh + 1) * half], qi[hh * half:(hh + 1) * half]
                a = jnp.zeros((half, chunk), F32)
                for jj in range((hh + 1) * half):
                    diff = bh - bi[jj:jj + 1]
                    if jj >= hh * half:
                        diff = jnp.where(hrow >= jj - hh * half, diff, NEG_INF)
                    pair = jnp.sum(qh * jnp.exp2(diff) * ki[jj:jj + 1], axis=-1, keepdims=True)
                    a = jnp.where(hcol == lo + jj, pair, a)
                blocks.append(a)
            if i > 0:
                ref_b = bi[0:1]
                qs = (qi * jnp.exp2(bi - ref_b)).astype(BF16)
                ks = (k * jnp.exp2(jnp.where(krow < lo, ref_b - b, NEG_INF))).astype(BF16)
                far = _dot_nt(qs, ks)
                blocks[-2] = blocks[-2] + far[:half]
                blocks[-1] = blocks[-1] + far[half:]
        attn = jnp.concatenate(blocks, axis=0)
        o = o + _dot(attn.astype(BF16), v)

        k_dec = (k * jnp.exp2(b_last - b)).astype(BF16)
        st_ref[...] = st * jnp.exp2(b_last) + _dot_tn(v, k_dec)

        ms = jnp.mean(o * o, axis=-1, keepdims=True)
        y = o * lax.rsqrt(ms + NORM_EPS) * hn
        r = r_ref[c0:c0 + chunk, :]
        o_ref[c0:c0 + chunk, :] = (y * (r * jax.nn.sigmoid(r))).astype(o_ref.dtype)

    @pl.when(ci == pl.num_programs(2) - 1)
    def _():
        s_ref[0, 0] = st_ref[...].T


def _gla_core(proj, g, head_norm, s0, seq, n_valid):
    m = proj.shape[0]
    bsz, heads, dk, dv = s0.shape
    chunk = GLA_CHUNK
    assert seq % chunk == 0 and chunk % GLA_SUB == 0 and GLA_SUB % 16 == 0 and (2 * heads * dk) % dv == 0
    span = chunk * _pick(seq // chunk, (GLA_CHUNKS_PER_STEP, 2, 1))
    nc = seq // span
    kb0 = heads
    vb0 = 2 * heads * dk // dv
    rb0 = vb0 + heads

    def rows(bi, hi, ci):
        return bi * nc + ci

    return pl.pallas_call(
        functools.partial(_gla_kernel, chunk=chunk, sub=GLA_SUB, n_valid=n_valid),
        grid=(bsz, heads, nc),
        in_specs=[pl.BlockSpec((span, dk), lambda bi, hi, ci: (rows(bi, hi, ci), hi)),
                  pl.BlockSpec((span, dk), lambda bi, hi, ci: (rows(bi, hi, ci), kb0 + hi)),
                  pl.BlockSpec((span, dv), lambda bi, hi, ci: (rows(bi, hi, ci), vb0 + hi)),
                  pl.BlockSpec((span, dk), lambda bi, hi, ci: (rows(bi, hi, ci), hi)),
                  pl.BlockSpec((span, dv), lambda bi, hi, ci: (rows(bi, hi, ci), rb0 + hi)),
                  pl.BlockSpec((1, dv), lambda bi, hi, ci: (0, 0)),
                  pl.BlockSpec((1, 1, dk, dv), lambda bi, hi, ci: (bi, hi, 0, 0))],
        out_specs=[pl.BlockSpec((span, dv), lambda bi, hi, ci: (rows(bi, hi, ci), hi)),
                   pl.BlockSpec((1, 1, dk, dv), lambda bi, hi, ci: (bi, hi, 0, 0))],
        out_shape=[jax.ShapeDtypeStruct((m, heads * dv), BF16),
                   jax.ShapeDtypeStruct((bsz, heads, dk, dv), F32)],
        scratch_shapes=[pltpu.VMEM((dv, dk), F32)],
        compiler_params=_params("arbitrary", "arbitrary", "arbitrary"),
        name="gla_core",
    )(proj, proj, proj, g, proj, head_norm.reshape(1, dv), s0)


def _gla_mixer(x, h, s0, seq, n_valid, layer, w_in, w_gate, b_gate, head_norm, w_out):
    heads, dk, dv = s0.shape[1:]
    n_main = 2 * heads * dk + 2 * heads * dv
    proj = _matmul(h, w_in, layer, n_main)
    g = _gla_gate(h, w_in, layer, n_main, w_gate, b_gate)
    o, s_fin = _gla_core(proj, g, head_norm, s0, seq, n_valid)
    return _matmul(o, w_out, layer, w_out.shape[2], res=x), s_fin


def _moba_prompt_kernel(q_ref, k_ref, v_ref, o_ref, kb_ref, vt_ref, km_ref, *, blk, top_k):
    seq, hd = q_ref.shape
    nb = seq // blk
    score_scale = (hd ** -0.5) * LOG2_E

    for n in range(nb):
        kblk = k_ref[n * blk:(n + 1) * blk, :]
        km_ref[n:n + 1, :] = jnp.mean(kblk, axis=0, keepdims=True)
        kb_ref[n * blk:(n + 1) * blk, :] = kblk.astype(BF16)
        vt_ref[:, n * blk:(n + 1) * blk] = v_ref[n * blk:(n + 1) * blk, :].T.astype(BF16)
    km_hi, km_lo = _split2(km_ref[...])

    brow = lax.broadcasted_iota(jnp.int32, (nb, blk), 0)
    key = lax.broadcasted_iota(jnp.int32, (blk, blk), 0)
    qry = lax.broadcasted_iota(jnp.int32, (blk, blk), 1)
    causal = jnp.where(key <= qry, 0.0, NEG_INF)

    for i in range(nb):
        q = q_ref[i * blk:(i + 1) * blk, :]
        qb = q.astype(BF16)
        s = _dot_nt(kb_ref[0:(i + 1) * blk, :], qb) * score_scale
        pieces = [s[i * blk:(i + 1) * blk] + causal]
        if i > 0:
            q_hi, q_lo = _split2(q)
            gate = _dot_nt(km_hi, q_hi) + _dot_nt(km_lo, q_hi) + _dot_nt(km_hi, q_lo)
            past = brow < i
            gate = jnp.where(past, gate, NEG_INF)
            rank = jnp.zeros((nb, blk), jnp.int32)
            for n in range(i):
                gn = gate[n:n + 1, :]
                rank = rank + jnp.where(gn > gate, 1, jnp.where(gn == gate, jnp.where(brow > n, 1, 0), 0))
            bias = jnp.where(past, jnp.where(rank < top_k, 0.0, NEG_INF), NEG_INF)
            pieces += [s[n * blk:(n + 1) * blk] + bias[n:n + 1] for n in range(i)]
        m = functools.reduce(jnp.maximum, [jnp.max(p, axis=0, keepdims=True) for p in pieces])
        probs = [jnp.exp2(p - m) for p in pieces]
        l = functools.reduce(jnp.add, [jnp.sum(p, axis=0, keepdims=True) for p in probs])
        blocks = [i] + list(range(i))
        acc = functools.reduce(jnp.add, [_dot(vt_ref[:, n * blk:(n + 1) * blk], p.astype(BF16))
                                         for n, p in zip(blocks, probs)])
        o_ref[i * blk:(i + 1) * blk, :] = (acc / l).T.astype(o_ref.dtype)


def _moba_prompt(q, k, v, bsz, seq, heads):
    m, d = q.shape
    hd = d // heads
    blk = MOBA_BLOCK
    assert seq % blk == 0 and hd == V7X_LANES
    nb = seq // blk
    spec = pl.BlockSpec((seq, hd), lambda bi, hi: (bi, hi))
    return pl.pallas_call(
        functools.partial(_moba_prompt_kernel, blk=blk, top_k=MOBA_TOP_K),
        grid=(bsz, heads),
        in_specs=[spec, spec, spec],
        out_specs=spec,
        out_shape=jax.ShapeDtypeStruct((m, d), BF16),
        scratch_shapes=[pltpu.VMEM((seq, hd), BF16),
                        pltpu.VMEM((hd, seq), BF16),
                        pltpu.VMEM((nb, hd), F32)],
        compiler_params=_params("parallel", "parallel"),
        name="moba_prompt",
    )(q, k, v)


KMEAN_BLOCKS_PER_STEP = 4
DECODE_Q_ROWS = 16


def _kmean_kernel(pt_ref, *refs, pages_per_block):
    *page_refs, o_ref = refs
    n = pl.program_id(2)
    blocks = len(page_refs) // pages_per_block
    rows = pages_per_block * page_refs[0].shape[2]
    for a in range(blocks):
        pages = page_refs[a * pages_per_block:(a + 1) * pages_per_block]
        total = functools.reduce(jnp.add, [jnp.sum(p[0, 0], axis=0) for p in pages])
        o_ref[0, 0, n * blocks + a] = total * (1.0 / rows)


def _past_block_means(cache, page_table):
    bsz, n_pages = page_table.shape
    layers, _, page, heads, hd = cache.shape
    ppb = MOBA_BLOCK // page
    nbp = n_pages // ppb
    bps = _pick(nbp, (KMEAN_BLOCKS_PER_STEP, 2, 1))

    def page_spec(a):
        return pl.BlockSpec((1, 1, page, heads, hd),
                            lambda li, bi, n, pt: (li, pt[bi, n * (bps * ppb) + a], 0, 0, 0))

    grid_spec = pltpu.PrefetchScalarGridSpec(
        num_scalar_prefetch=1,
        grid=(layers, bsz, nbp // bps),
        in_specs=[page_spec(a) for a in range(bps * ppb)],
        out_specs=pl.BlockSpec((1, 1, nbp, heads, hd), lambda li, bi, n, pt: (li, bi, 0, 0, 0)),
    )
    return pl.pallas_call(
        functools.partial(_kmean_kernel, pages_per_block=ppb),
        grid_spec=grid_spec,
        out_shape=jax.ShapeDtypeStruct((layers, bsz, nbp, heads, hd), F32),
        compiler_params=_params("arbitrary", "arbitrary", "arbitrary"),
        name="moba_kmean",
    )(page_table, *([cache] * (bps * ppb)))


def _moba_select_kernel(q_ref, km_ref, o_ref, *, top_k, n_q):
    km = km_ref[0]
    nbp, heads, _ = km.shape
    bidx = lax.broadcasted_iota(jnp.int32, (nbp, heads, 1), 0)
    for t in range(n_q):
        gate = jnp.sum(km * q_ref[0, t][None], axis=-1, keepdims=True)
        for r in range(top_k):
            best = jnp.max(gate, axis=0, keepdims=True)
            idx = jnp.min(jnp.where(gate == best, bidx, nbp), axis=0, keepdims=True)
            o_ref[0, t * top_k + r] = jnp.broadcast_to(idx[0], o_ref.shape[2:])
            gate = jnp.where(bidx == idx, NEG_INF, gate)


def _moba_select(q, kmean, layer, n_q):
    bsz, _, heads, hd = q.shape
    nbp = kmean.shape[2]
    slots = n_q * MOBA_TOP_K
    sel = pl.pallas_call(
        functools.partial(_moba_select_kernel, top_k=MOBA_TOP_K, n_q=n_q),
        grid=(bsz,),
        in_specs=[pl.BlockSpec((1, n_q, heads, hd), lambda bi: (bi, 0, 0, 0)),
                  pl.BlockSpec((None, 1, nbp, heads, hd), lambda bi: (layer, bi, 0, 0, 0))],
        out_specs=pl.BlockSpec((1, slots, heads, V7X_LANES), lambda bi: (bi, 0, 0, 0)),
        out_shape=jax.ShapeDtypeStruct((bsz, slots, heads, V7X_LANES), jnp.int32),
        compiler_params=_params("parallel"),
        name="moba_select",
    )(q, kmean)
    return sel[..., 0]


def _moba_decode_kernel(pt_ref, sel_ref, q_ref, kn_ref, vn_ref, kc_hbm, vc_hbm, o_ref,
                        kbuf, vbuf, sems, *, layer, heads, top_k, pages_per_block, n_q):
    b = pl.program_id(0)
    h = pl.program_id(1)
    step = b * heads + h
    n_steps = pl.num_programs(0) * heads
    cur = lax.rem(step, 2)
    n_slots = n_q * top_k
    n_pages = n_slots * pages_per_block
    rows, hd = q_ref.shape[1:]
    page = kn_ref.shape[1]
    scale = hd ** -0.5
    per_query = top_k * pages_per_block

    def page_copies(bb, hh, half, i):
        slot, pg = divmod(i, pages_per_block)
        t, j = divmod(i, per_query)
        blk = sel_ref[(bb * n_slots + slot) * heads + hh]
        phys = pt_ref[bb, blk * pages_per_block + pg]
        dst = pl.ds(j * page, page)
        return (pltpu.make_async_copy(kc_hbm.at[layer, phys, :, hh, :], kbuf.at[half, t, dst], sems.at[0, half, i]),
                pltpu.make_async_copy(vc_hbm.at[layer, phys, :, hh, :], vbuf.at[half, t, dst], sems.at[1, half, i]))

    def start_gather(bb, hh, half):
        for i in range(n_pages):
            for c in page_copies(bb, hh, half, i):
                c.start()

    @pl.when(step == 0)
    def _():
        start_gather(b, h, cur)

    @pl.when(step + 1 < n_steps)
    def _():
        nxt = step + 1
        start_gather(nxt // heads, lax.rem(nxt, heads), 1 - cur)

    for i in range(n_pages):
        for c in page_copies(b, h, cur, i):
            c.wait()

    qb = q_ref[0].astype(BF16)
    lane = lax.broadcasted_iota(jnp.int32, (1, page), 1)
    rowi = lax.broadcasted_iota(jnp.int32, (rows, 1), 0)
    s_own = _dot_nt(qb, kn_ref[0].astype(BF16)) * scale
    v_own = vn_ref[0].astype(BF16)
    s_past = [_dot_nt(qb, kbuf[cur, t].astype(BF16)) * scale for t in range(n_q)]
    s_new = [jnp.where(lane <= t, s_own, NEG_INF) for t in range(n_q)]
    m = [jnp.maximum(jnp.max(sp, axis=1, keepdims=True), jnp.max(sn, axis=1, keepdims=True))
         for sp, sn in zip(s_past, s_new)]
    p_past = [jnp.exp(sp - mt) for sp, mt in zip(s_past, m)]
    p_new = [jnp.exp(sn - mt) for sn, mt in zip(s_new, m)]
    l = [jnp.sum(pp, axis=1, keepdims=True) + jnp.sum(pn, axis=1, keepdims=True)
         for pp, pn in zip(p_past, p_new)]
    acc = [_dot(pp.astype(BF16), vbuf[cur, t].astype(BF16)) + _dot(pn.astype(BF16), v_own)
           for t, (pp, pn) in enumerate(zip(p_past, p_new))]
    out = jnp.zeros((rows, hd), F32)
    for t in range(n_q):
        out = jnp.where(rowi == t, acc[t] / l[t], out)
    o_ref[0] = out


def _moba_decode(q, k_new, v_new, cache_k, cache_v, page_table, sel, layer, n_q):
    bsz, rows, d = q.shape
    _, _, page, heads, hd = cache_k.shape
    ppb = MOBA_BLOCK // page
    top_k = MOBA_TOP_K
    n_pages = n_q * top_k * ppb
    assert n_q <= rows and k_new.shape[1] == page
    q_spec = pl.BlockSpec((1, rows, hd), lambda bi, hi, pt, sl: (bi, 0, hi))
    new_spec = pl.BlockSpec((1, page, hd), lambda bi, hi, pt, sl: (bi, 0, hi))
    any_spec = pl.BlockSpec(memory_space=pl.ANY)
    grid_spec = pltpu.PrefetchScalarGridSpec(
        num_scalar_prefetch=2,
        grid=(bsz, heads),
        in_specs=[q_spec, new_spec, new_spec, any_spec, any_spec],
        out_specs=q_spec,
        scratch_shapes=[pltpu.VMEM((2, n_q, top_k * ppb * page, hd), F32),
                        pltpu.VMEM((2, n_q, top_k * ppb * page, hd), F32),
                        pltpu.SemaphoreType.DMA((2, 2, n_pages))],
    )
    return pl.pallas_call(
        functools.partial(_moba_decode_kernel, layer=layer, heads=heads, top_k=top_k,
                          pages_per_block=ppb, n_q=n_q),
        grid_spec=grid_spec,
        out_shape=jax.ShapeDtypeStruct((bsz, rows, d), F32),
        compiler_params=_params("arbitrary", "arbitrary"),
        name="moba_decode",
    )(page_table, sel.reshape(-1), q, k_new, v_new, cache_k, cache_v)


def kernel(x_prompt, x_sample, state_gla, cache_k, cache_v, page_table, norm_mix, norm_ffn, norm_final,
           gla_w_in, gla_w_gate, gla_b_gate, gla_head_norm, gla_w_out, moba_w_qkv, moba_w_o,
           ffn_w_gate_up, ffn_w_down):
    bp, seq, d = x_prompt.shape
    bd, sd, _ = x_sample.shape
    depth = norm_mix.shape[0]
    n_moba, n_pool, page, mh, mhd = cache_k.shape
    past_len = page_table.shape[1] * page
    assert past_len % MOBA_BLOCK == 0 and sd <= min(MOBA_BLOCK, 8) and GLA_CHUNK % sd == 0
    gh, gdk, gdv = state_gla.shape[2:]

    xp = x_prompt.reshape(bp * seq, d)
    xs = x_sample.reshape(bd * sd, d)
    rope_p = _rope_tables(seq, 0, seq, mhd)
    rope_s = _rope_tables(bd * sd, past_len, sd, mhd)
    kmean = _past_block_means(cache_k, page_table)
    zero_state = jnp.zeros((bp, gh, gdk, gdv), F32)
    pad_s = GLA_CHUNK - sd

    def pad_tokens(a, n):
        c = a.shape[1]
        return jnp.pad(a.reshape(bd, sd, c), ((0, 0), (0, n), (0, 0))).reshape(bd * (sd + n), c)

    gla_sp, gla_ss, kp_l, vp_l, ks_l, vs_l = [], [], [], [], [], []
    for i in range(depth):
        j = i // 2
        hp = _rmsnorm(xp, norm_mix[i], BF16)
        hs = _rmsnorm(xs, norm_mix[i], BF16)
        if i % 2 == 0:
            args = (j, gla_w_in, gla_w_gate[j], gla_b_gate[j], gla_head_norm[j], gla_w_out)
            xp, sp = _gla_mixer(xp, hp, zero_state, seq, seq, *args)
            xs_pad, ss = _gla_mixer(pad_tokens(xs, pad_s), pad_tokens(hs, pad_s),
                                    state_gla[j], sd + pad_s, sd, *args)
            xs = xs_pad.reshape(bd, sd + pad_s, d)[:, :sd].reshape(bd * sd, d)
            gla_sp.append(sp)
            gla_ss.append(ss)
        else:
            w_qkv, w_o = moba_w_qkv, moba_w_o
            qp = _matmul(hp, w_qkv, j, d, 0, rope=rope_p)
            kp = _matmul(hp, w_qkv, j, d, d, rope=rope_p)
            vp = _matmul(hp, w_qkv, j, d, 2 * d)
            op = _moba_prompt(qp, kp, vp, bp, seq, mh)
            xp = _matmul(op, w_o, j, d, res=xp)
            kp_l.append(kp.reshape(bp * seq // page, page, mh, mhd))
            vp_l.append(vp.reshape(bp * seq // page, page, mh, mhd))

            qs = _matmul(hs, w_qkv, j, d, 0, rope=rope_s)
            ks = _matmul(hs, w_qkv, j, d, d, rope=rope_s)
            vs = _matmul(hs, w_qkv, j, d, 2 * d)
            q3 = pad_tokens(qs, DECODE_Q_ROWS - sd).reshape(bd, DECODE_Q_ROWS, d)
            k3, v3 = (pad_tokens(a, page - sd).reshape(bd, page, d) for a in (ks, vs))
            sel = _moba_select(qs.reshape(bd, sd, mh, mhd), kmean, j, sd)
            o3 = _moba_decode(q3, k3, v3, cache_k, cache_v, page_table, sel, j, sd)
            xs = _matmul(o3[:, :sd].reshape(bd * sd, d), w_o, j, d, res=xs)
            ks_l.append(ks.reshape(bd, sd, mh, mhd))
            vs_l.append(vs.reshape(bd, sd, mh, mhd))
        xp = _ffn(xp, norm_ffn[i], ffn_w_gate_up, ffn_w_down, i)
        xs = _ffn(xs, norm_ffn[i], ffn_w_gate_up, ffn_w_down, i)

    y_prompt = _rmsnorm(xp, norm_final, F32).reshape(bp, seq, d)
    y_sample = _rmsnorm(xs, norm_final, F32).reshape(bd, sd, d)
    return (y_prompt, y_sample, jnp.stack(gla_sp), jnp.stack(gla_ss),
            jnp.stack(kp_l), jnp.stack(vp_l), jnp.stack(ks_l), jnp.stack(vs_l))
```

```python
import functools

import jax
import jax.numpy as jnp
from jax import lax
from jax.experimental import pallas as pl
from jax.experimental.pallas import tpu as pltpu

NORM_EPS = 1e-6
GLA_HEADS = 4
GLA_GATE_RANK = 16
GLA_GATE_NORMALIZER = 16.0
GLA_CHUNK = 64
GLA_SUB = 16
GLA_CHUNKS_PER_STEP = 4
MOBA_HEADS = 16
MOBA_BLOCK = 256
MOBA_TOP_K = 3
ROPE_THETA = 500000.0
ROPE_FRACTION = 4
PAGE_SIZE = 128

V7X_LANES = 128
V7X_VMEM_LIMIT_BYTES = 56 * 1024 * 1024

F32 = jnp.float32
BF16 = jnp.bfloat16
NEG_INF = float("-inf")
LOG2_E = 1.4426950408889634


def _params(*sem):
    return pltpu.CompilerParams(dimension_semantics=sem, vmem_limit_bytes=V7X_VMEM_LIMIT_BYTES)


def _pick(n, candidates):
    for c in candidates:
        if n % c == 0:
            return c
    raise ValueError(f"no tile for {n} in {candidates}")


def _dot(a, b):
    return jnp.dot(a, b, preferred_element_type=F32)


def _dot_nt(a, b):
    return lax.dot_general(a, b, (((1,), (1,)), ((), ())), preferred_element_type=F32)


def _dot_tn(a, b):
    return lax.dot_general(a, b, (((0,), (0,)), ((), ())), preferred_element_type=F32)


def _split2(x):
    hi = x.astype(BF16)
    lo = (x - hi.astype(F32)).astype(BF16)
    return hi, lo


def _split3(x):
    hi = x.astype(BF16)
    r1 = x - hi.astype(F32)
    mid = r1.astype(BF16)
    lo = (r1 - mid.astype(F32)).astype(BF16)
    return hi, mid, lo


def _rmsnorm_kernel(x_ref, g_ref, o_ref):
    x = x_ref[...]
    ms = jnp.mean(x * x, axis=-1, keepdims=True)
    o_ref[...] = (x * lax.rsqrt(ms + NORM_EPS) * g_ref[...]).astype(o_ref.dtype)


def _rmsnorm(x, g, out_dtype):
    m, d = x.shape
    tm = _pick(m, (512, 256, 128, 64, 32, 16))
    return pl.pallas_call(
        _rmsnorm_kernel,
        grid=(m // tm,),
        in_specs=[pl.BlockSpec((tm, d), lambda i: (i, 0)),
                  pl.BlockSpec((1, d), lambda i: (0, 0))],
        out_specs=pl.BlockSpec((tm, d), lambda i: (i, 0)),
        out_shape=jax.ShapeDtypeStruct((m, d), out_dtype),
        compiler_params=_params("parallel"),
        name="rmsnorm",
    )(x, g.reshape(1, d))


def _rope(acc, cos_ref, sin_ref, rot_half):
    tn = acc.shape[1]
    reps = tn // V7X_LANES
    cos = jnp.concatenate([cos_ref[...]] * reps, axis=1)
    sin = jnp.concatenate([sin_ref[...]] * reps, axis=1)
    lane = lax.broadcasted_iota(jnp.int32, acc.shape, 1) % V7X_LANES
    partner = jnp.where(lane < rot_half,
                        pltpu.roll(acc, tn - rot_half, 1),
                        pltpu.roll(acc, rot_half, 1))
    return acc * cos + partner * sin


def _mm_kernel(*refs, kind, rot_half):
    n_extra = {"plain": 0, "res": 1, "rope": 2}[kind]
    x_ref, w_ref = refs[:2]
    extras = refs[2:2 + n_extra]
    xd_ref = refs[2 + n_extra]
    extras_d = refs[3 + n_extra:3 + 2 * n_extra]
    o_ref, od_ref, wb_ref = refs[3 + 2 * n_extra:]

    def epilogue(acc, ex):
        if kind == "res":
            return ex[0][...] + acc
        if kind == "rope":
            return _rope(acc, ex[0], ex[1], rot_half)
        return acc

    @pl.when(pl.program_id(1) == 0)
    def _():
        wb_ref[...] = w_ref[...].astype(BF16)
        od_ref[...] = epilogue(_dot(xd_ref[...].astype(BF16), wb_ref[...]), extras_d).astype(od_ref.dtype)

    o_ref[...] = epilogue(_dot(x_ref[...].astype(BF16), wb_ref[...]), extras).astype(o_ref.dtype)


def _mm_swiglu_kernel(x_ref, wg_ref, wu_ref, xd_ref, o_ref, od_ref, wgb_ref, wub_ref):
    def gated(x):
        g = _dot(x, wgb_ref[...])
        u = _dot(x, wub_ref[...])
        return g * jax.nn.sigmoid(g) * u

    @pl.when(pl.program_id(1) == 0)
    def _():
        wgb_ref[...] = wg_ref[...].astype(BF16)
        wub_ref[...] = wu_ref[...].astype(BF16)
        od_ref[...] = gated(xd_ref[...]).astype(od_ref.dtype)

    o_ref[...] = gated(x_ref[...]).astype(o_ref.dtype)


def _mm_tiles(m, k, n):
    tm = _pick(m, (1024, 512, 256, 128, 64, 32, 16))
    tn = _pick(n, (1024, 512, 256, 128))
    if k > 4096:
        tm, tn = min(tm, 512), min(tn, 512)
    return tm, tn


def _matmul(x, xd, w, layer, n, col0=0, out_dtype=F32, res=None, rope=None):
    m, k = x.shape
    md = xd.shape[0]
    tm, tn = _mm_tiles(m, k, n)
    assert col0 % tn == 0 and w.shape[1] == k and xd.shape[1] == k
    cb0 = col0 // tn
    x_spec = pl.BlockSpec((tm, k), lambda j, i: (i, 0))
    xd_spec = pl.BlockSpec((md, k), lambda j, i: (0, 0))
    w_spec = pl.BlockSpec((None, k, tn), lambda j, i: (layer, 0, cb0 + j))
    o_spec = pl.BlockSpec((tm, tn), lambda j, i: (i, j))
    od_spec = pl.BlockSpec((md, tn), lambda j, i: (0, j))
    kind, rot_half, ex, ex_specs, exd, exd_specs = "plain", 0, (), [], (), []
    if res is not None:
        kind, ex, ex_specs, exd, exd_specs = "res", (res[0],), [o_spec], (res[1],), [od_spec]
    elif rope is not None:
        (cos, sin, rot_half), (cos_d, sin_d, _) = rope
        assert cos.shape[0] % tm == 0 and cos_d.shape[0] == md
        period = cos.shape[0] // tm
        kind = "rope"
        ex, ex_specs = (cos, sin), [pl.BlockSpec((tm, V7X_LANES), lambda j, i: (i % period, 0))] * 2
        exd, exd_specs = (cos_d, sin_d), [pl.BlockSpec((md, V7X_LANES), lambda j, i: (0, 0))] * 2
    return pl.pallas_call(
        functools.partial(_mm_kernel, kind=kind, rot_half=rot_half),
        grid=(n // tn, m // tm),
        in_specs=[x_spec, w_spec, *ex_specs, xd_spec, *exd_specs],
        out_specs=[o_spec, od_spec],
        out_shape=[jax.ShapeDtypeStruct((m, n), out_dtype), jax.ShapeDtypeStruct((md, n), out_dtype)],
        scratch_shapes=[pltpu.VMEM((k, tn), BF16)],
        compiler_params=_params("arbitrary", "arbitrary"),
        name="matmul",
    )(x, w, *ex, xd, *exd)


def _swiglu_up(x, xd, w_gate_up, layer):
    m, k = x.shape
    md = xd.shape[0]
    f = w_gate_up.shape[2] // 2
    tm, tn = _mm_tiles(m, k, f)
    nb = f // tn
    return pl.pallas_call(
        _mm_swiglu_kernel,
        grid=(nb, m // tm),
        in_specs=[pl.BlockSpec((tm, k), lambda j, i: (i, 0)),
                  pl.BlockSpec((None, k, tn), lambda j, i: (layer, 0, j)),
                  pl.BlockSpec((None, k, tn), lambda j, i: (layer, 0, nb + j)),
                  pl.BlockSpec((md, k), lambda j, i: (0, 0))],
        out_specs=[pl.BlockSpec((tm, tn), lambda j, i: (i, j)),
                   pl.BlockSpec((md, tn), lambda j, i: (0, j))],
        out_shape=[jax.ShapeDtypeStruct((m, f), BF16), jax.ShapeDtypeStruct((md, f), BF16)],
        scratch_shapes=[pltpu.VMEM((k, tn), BF16), pltpu.VMEM((k, tn), BF16)],
        compiler_params=_params("arbitrary", "arbitrary"),
        name="swiglu_up",
    )(x, w_gate_up, w_gate_up, xd)


def _ffn(x, xd, g, w_gate_up, w_down, layer):
    h, hdec = _rmsnorm(x, g, BF16), _rmsnorm(xd, g, BF16)
    hid, hid_d = _swiglu_up(h, hdec, w_gate_up, layer)
    return _matmul(hid, hid_d, w_down, layer, w_down.shape[2], res=(x, xd))


def _rope_table_kernel(inv_ref, cos_ref, sin_ref, *, base, period, rot_half):
    rows = cos_ref.shape[0]
    row = lax.broadcasted_iota(jnp.int32, (rows, V7X_LANES), 0) + pl.program_id(0) * rows
    lane = lax.broadcasted_iota(jnp.int32, (rows, V7X_LANES), 1)
    pos = (base + (row & (period - 1))).astype(F32)
    ang = pos * inv_ref[...]
    c, s = jnp.cos(ang), jnp.sin(ang)
    cos_ref[...] = jnp.where(lane < 2 * rot_half, c, 1.0)
    sin_ref[...] = jnp.where(lane < rot_half, -s, jnp.where(lane < 2 * rot_half, s, 0.0))


def _rope_tables(rows, base, period, head_dim):
    assert period & (period - 1) == 0 and head_dim == V7X_LANES
    rot_half = head_dim // ROPE_FRACTION // 2
    inv = ROPE_THETA ** (-jnp.arange(rot_half, dtype=F32) / rot_half)
    inv = jnp.concatenate([inv, inv, jnp.zeros((head_dim - 2 * rot_half,), F32)]).reshape(1, head_dim)
    tr = _pick(rows, (256, 128, 64, 32, 16, 8))
    spec = pl.BlockSpec((tr, head_dim), lambda i: (i, 0))
    cos, sin = pl.pallas_call(
        functools.partial(_rope_table_kernel, base=base, period=period, rot_half=rot_half),
        grid=(rows // tr,),
        in_specs=[pl.BlockSpec((1, head_dim), lambda i: (0, 0))],
        out_specs=[spec, spec],
        out_shape=[jax.ShapeDtypeStruct((rows, head_dim), F32)] * 2,
        compiler_params=_params("parallel"),
        name="rope_tables",
    )(inv)
    return cos, sin, rot_half


def _gla_gate_kernel(h_ref, wlr_ref, wg_ref, b_ref, o_ref):
    rank = wg_ref.shape[0]
    glr = _dot(h_ref[...], wlr_ref[:, :rank].astype(BF16))
    z = _dot(glr.astype(BF16), wg_ref[...].astype(BF16)) + b_ref[...]
    o_ref[...] = (jnp.minimum(z, 0.0) - jnp.log1p(jnp.exp(-jnp.abs(z)))) * (1.0 / GLA_GATE_NORMALIZER)


def _gla_gate(h, w_in, layer, col0, w_gate, b_gate):
    m, d = h.shape
    r, kd = w_gate.shape
    assert col0 % V7X_LANES == 0 and col0 + r == w_in.shape[2] and r <= V7X_LANES
    tm = _pick(m, (512, 256, 128, 64, 32, 16))
    return pl.pallas_call(
        _gla_gate_kernel,
        grid=(m // tm,),
        in_specs=[pl.BlockSpec((tm, d), lambda i: (i, 0)),
                  pl.BlockSpec((None, d, V7X_LANES), lambda i: (layer, 0, col0 // V7X_LANES)),
                  pl.BlockSpec((r, kd), lambda i: (0, 0)),
                  pl.BlockSpec((1, kd), lambda i: (0, 0))],
        out_specs=pl.BlockSpec((tm, kd), lambda i: (i, 0)),
        out_shape=jax.ShapeDtypeStruct((m, kd), F32),
        compiler_params=_params("parallel"),
        name="gla_gate",
    )(h, w_in, w_gate, b_gate.reshape(1, kd))


def _gla_kernel(q_ref, k_ref, v_ref, g_ref, r_ref, hn_ref, *rest, chunk, sub, n_valid):
    s0_ref = rest[0] if len(rest) == 4 else None
    o_ref, s_ref, st_ref = rest[-3:]
    ci = pl.program_id(2)
    span, dk = q_ref.shape
    half = sub // 2

    @pl.when(ci == 0)
    def _():
        if s0_ref is None:
            st_ref[...] = jnp.zeros(st_ref.shape, st_ref.dtype)
        else:
            st_ref[...] = s0_ref[0, 0].T

    row = lax.broadcasted_iota(jnp.int32, (chunk, chunk), 0)
    col = lax.broadcasted_iota(jnp.int32, (chunk, chunk), 1)
    tri = jnp.where(col <= row, 1.0, 0.0).astype(BF16)
    krow = lax.broadcasted_iota(jnp.int32, (chunk, 1), 0)
    hrow = lax.broadcasted_iota(jnp.int32, (half, 1), 0)
    hcol = lax.broadcasted_iota(jnp.int32, (half, chunk), 1)
    hn = hn_ref[...]

    for c0 in range(0, span, chunk):
        q = q_ref[c0:c0 + chunk, :] * (dk ** -0.5)
        k = k_ref[c0:c0 + chunk, :]
        v = v_ref[c0:c0 + chunk, :].astype(BF16)
        tok = krow + (ci * span + c0)
        g = jnp.where(tok < n_valid, g_ref[c0:c0 + chunk, :], 0.0) * LOG2_E
        g_hi, g_mid, g_lo = _split3(g)
        b = _dot(tri, g_hi) + _dot(tri, g_mid) + _dot(tri, g_lo)
        b_last = b[chunk - 1:chunk, :]

        st = st_ref[...]
        o = _dot_nt((q * jnp.exp2(b)).astype(BF16), st.astype(BF16))

        blocks = []
        for i in range(chunk // sub):
            lo = i * sub
            bi, qi, ki = b[lo:lo + sub], q[lo:lo + sub], k[lo:lo + sub]
            for hh in range(2):
                bh, qh = bi[hh * half:(hh + 1) * half], qi[hh * half:(hh + 1) * half]
                a = jnp.zeros((half, chunk), F32)
                for jj in range((hh + 1) * half):
                    diff = bh - bi[jj:jj + 1]
                    if jj >= hh * half:
                        diff = jnp.where(hrow >= jj - hh * half, diff, NEG_INF)
                    pair = jnp.sum(qh * jnp.exp2(diff) * ki[jj:jj + 1], axis=-1, keepdims=True)
                    a = jnp.where(hcol == lo + jj, pair, a)
                blocks.append(a)
            if i > 0:
                ref_b = bi[0:1]
                qs = (qi * jnp.exp2(bi - ref_b)).astype(BF16)
                ks = (k * jnp.exp2(jnp.where(krow < lo, ref_b - b, NEG_INF))).astype(BF16)
                far = _dot_nt(qs, ks)
                blocks[-2] = blocks[-2] + far[:half]
                blocks[-1] = blocks[-1] + far[half:]
        attn = jnp.concatenate(blocks, axis=0)
        o = o + _dot(attn.astype(BF16), v)

        k_dec = (k * jnp.exp2(b_last - b)).astype(BF16)
        st_ref[...] = st * jnp.exp2(b_last) + _dot_tn(v, k_dec)

        ms = jnp.mean(o * o, axis=-1, keepdims=True)
        y = o * lax.rsqrt(ms + NORM_EPS) * hn
        r = r_ref[c0:c0 + chunk, :]
        o_ref[c0:c0 + chunk, :] = (y * (r * jax.nn.sigmoid(r))).astype(o_ref.dtype)

    @pl.when(ci == pl.num_programs(2) - 1)
    def _():
        s_ref[0, 0] = st_ref[...].T


def _gla_core(proj, g, head_norm, state, layer, dims, seq, n_valid):
    m = proj.shape[0]
    bsz, heads, dk, dv = dims
    chunk = GLA_CHUNK
    assert seq % chunk == 0 and chunk % GLA_SUB == 0 and GLA_SUB % 16 == 0 and (2 * heads * dk) % dv == 0
    span = chunk * _pick(seq // chunk, (GLA_CHUNKS_PER_STEP, 2, 1))
    nc = seq // span
    kb0 = heads
    vb0 = 2 * heads * dk // dv
    rb0 = vb0 + heads

    def rows(bi, hi, ci):
        return bi * nc + ci

    in_specs = [pl.BlockSpec((span, dk), lambda bi, hi, ci: (rows(bi, hi, ci), hi)),
                pl.BlockSpec((span, dk), lambda bi, hi, ci: (rows(bi, hi, ci), kb0 + hi)),
                pl.BlockSpec((span, dv), lambda bi, hi, ci: (rows(bi, hi, ci), vb0 + hi)),
                pl.BlockSpec((span, dk), lambda bi, hi, ci: (rows(bi, hi, ci), hi)),
                pl.BlockSpec((span, dv), lambda bi, hi, ci: (rows(bi, hi, ci), rb0 + hi)),
                pl.BlockSpec((1, dv), lambda bi, hi, ci: (0, 0))]
    operands = [proj, proj, proj, g, proj, head_norm.reshape(1, dv)]
    if state is not None:
        in_specs.append(pl.BlockSpec((None, 1, 1, dk, dv), lambda bi, hi, ci: (layer, bi, hi, 0, 0)))
        operands.append(state)
    return pl.pallas_call(
        functools.partial(_gla_kernel, chunk=chunk, sub=GLA_SUB, n_valid=n_valid),
        grid=(bsz, heads, nc),
        in_specs=in_specs,
        out_specs=[pl.BlockSpec((span, dv), lambda bi, hi, ci: (rows(bi, hi, ci), hi)),
                   pl.BlockSpec((1, 1, dk, dv), lambda bi, hi, ci: (bi, hi, 0, 0))],
        out_shape=[jax.ShapeDtypeStruct((m, heads * dv), BF16),
                   jax.ShapeDtypeStruct((bsz, heads, dk, dv), F32)],
        scratch_shapes=[pltpu.VMEM((dv, dk), F32)],
        compiler_params=_params("arbitrary", "arbitrary", "arbitrary"),
        name="gla_core",
    )(*operands)


def _moba_prompt_kernel(q_ref, k_ref, v_ref, o_ref, kb_ref, vt_ref, km_ref, *, blk, top_k):
    seq, hd = q_ref.shape
    nb = seq // blk
    score_scale = (hd ** -0.5) * LOG2_E

    for n in range(nb):
        kblk = k_ref[n * blk:(n + 1) * blk, :]
        km_ref[n:n + 1, :] = jnp.mean(kblk, axis=0, keepdims=True)
        kb_ref[n * blk:(n + 1) * blk, :] = kblk.astype(BF16)
        vt_ref[:, n * blk:(n + 1) * blk] = v_ref[n * blk:(n + 1) * blk, :].T.astype(BF16)
    km_hi, km_lo = _split2(km_ref[...])

    brow = lax.broadcasted_iota(jnp.int32, (nb, blk), 0)
    key = lax.broadcasted_iota(jnp.int32, (blk, blk), 0)
    qry = lax.broadcasted_iota(jnp.int32, (blk, blk), 1)
    causal = jnp.where(key <= qry, 0.0, NEG_INF)

    def scores(i):
        q = q_ref[i * blk:(i + 1) * blk, :]
        s = _dot_nt(kb_ref[0:(i + 1) * blk, :], q.astype(BF16)) * score_scale
        pieces = []
        if i > 0:
            q_hi, q_lo = _split2(q)
            gate = _dot_nt(km_hi, q_hi) + _dot_nt(km_lo, q_hi) + _dot_nt(km_hi, q_lo)
            past = brow < i
            gate = jnp.where(past, gate, NEG_INF)
            rank = jnp.zeros((nb, blk), jnp.int32)
            for n in range(i):
                gn = gate[n:n + 1, :]
                rank = rank + jnp.where(gn > gate, 1, jnp.where(gn == gate, jnp.where(brow > n, 1, 0), 0))
            bias = jnp.where(past, jnp.where(rank < top_k, 0.0, NEG_INF), NEG_INF)
            pieces = [s[n * blk:(n + 1) * blk] + bias[n:n + 1] for n in range(i)]
        return pieces + [s[i * blk:(i + 1) * blk] + causal]

    def softmax(pieces):
        m = functools.reduce(jnp.maximum, [jnp.max(p, axis=0, keepdims=True) for p in pieces])
        probs = [jnp.exp2(p - m) for p in pieces]
        l = functools.reduce(jnp.add, [jnp.sum(p, axis=0, keepdims=True) for p in probs])
        return jnp.concatenate([p.astype(BF16) for p in probs], axis=0), l

    def attend(i, probs, l):
        acc = _dot(vt_ref[:, 0:(i + 1) * blk], probs)
        o_ref[i * blk:(i + 1) * blk, :] = (acc / l).T.astype(o_ref.dtype)

    pending = scores(0)
    for i in range(nb):
        upcoming = scores(i + 1) if i + 1 < nb else None
        probs, l = softmax(pending)
        attend(i, probs, l)
        pending = upcoming


def _moba_prompt(q, k, v, bsz, seq, heads):
    m, d = q.shape
    hd = d // heads
    blk = MOBA_BLOCK
    assert seq % blk == 0 and hd == V7X_LANES
    nb = seq // blk
    spec = pl.BlockSpec((seq, hd), lambda bi, hi: (bi, hi))
    return pl.pallas_call(
        functools.partial(_moba_prompt_kernel, blk=blk, top_k=MOBA_TOP_K),
        grid=(bsz, heads),
        in_specs=[spec, spec, spec],
        out_specs=spec,
        out_shape=jax.ShapeDtypeStruct((m, d), BF16),
        scratch_shapes=[pltpu.VMEM((seq, hd), BF16),
                        pltpu.VMEM((hd, seq), BF16),
                        pltpu.VMEM((nb, hd), F32)],
        compiler_params=_params("parallel", "parallel"),
        name="moba_prompt",
    )(q, k, v)


KMEAN_BLOCKS_PER_STEP = 4
DECODE_Q_ROWS = 16


def _kmean_kernel(pt_ref, *refs, pages_per_block):
    *page_refs, o_ref = refs
    n = pl.program_id(2)
    blocks = len(page_refs) // pages_per_block
    rows = pages_per_block * page_refs[0].shape[2]
    for a in range(blocks):
        pages = page_refs[a * pages_per_block:(a + 1) * pages_per_block]
        total = functools.reduce(jnp.add, [jnp.sum(p[0, 0], axis=0) for p in pages])
        o_ref[0, 0, n * blocks + a] = total * (1.0 / rows)


def _past_block_means(cache, page_table):
    bsz, n_pages = page_table.shape
    layers, _, page, heads, hd = cache.shape
    ppb = MOBA_BLOCK // page
    nbp = n_pages // ppb
    bps = _pick(nbp, (KMEAN_BLOCKS_PER_STEP, 2, 1))

    def page_spec(a):
        return pl.BlockSpec((1, 1, page, heads, hd),
                            lambda li, bi, n, pt: (li, pt[bi, n * (bps * ppb) + a], 0, 0, 0))

    grid_spec = pltpu.PrefetchScalarGridSpec(
        num_scalar_prefetch=1,
        grid=(layers, bsz, nbp // bps),
        in_specs=[page_spec(a) for a in range(bps * ppb)],
        out_specs=pl.BlockSpec((1, 1, nbp, heads, hd), lambda li, bi, n, pt: (li, bi, 0, 0, 0)),
    )
    return pl.pallas_call(
        functools.partial(_kmean_kernel, pages_per_block=ppb),
        grid_spec=grid_spec,
        out_shape=jax.ShapeDtypeStruct((layers, bsz, nbp, heads, hd), F32),
        compiler_params=_params("arbitrary", "arbitrary", "arbitrary"),
        name="moba_kmean",
    )(page_table, *([cache] * (bps * ppb)))


def _moba_select_kernel(q_ref, km_ref, o_ref, *, top_k, n_q):
    km = km_ref[0]
    nbp, heads, _ = km.shape
    bidx = lax.broadcasted_iota(jnp.int32, (nbp, heads, 1), 0)
    for t in range(n_q):
        gate = jnp.sum(km * q_ref[0, t][None], axis=-1, keepdims=True)
        for r in range(top_k):
            best = jnp.max(gate, axis=0, keepdims=True)
            idx = jnp.min(jnp.where(gate == best, bidx, nbp), axis=0, keepdims=True)
            o_ref[0, t * top_k + r] = jnp.broadcast_to(idx[0], o_ref.shape[2:])
            gate = jnp.where(bidx == idx, NEG_INF, gate)


def _moba_select(q, kmean, layer, n_q):
    bsz, _, heads, hd = q.shape
    nbp = kmean.shape[2]
    slots = n_q * MOBA_TOP_K
    sel = pl.pallas_call(
        functools.partial(_moba_select_kernel, top_k=MOBA_TOP_K, n_q=n_q),
        grid=(bsz,),
        in_specs=[pl.BlockSpec((1, n_q, heads, hd), lambda bi: (bi, 0, 0, 0)),
                  pl.BlockSpec((None, 1, nbp, heads, hd), lambda bi: (layer, bi, 0, 0, 0))],
        out_specs=pl.BlockSpec((1, slots, heads, V7X_LANES), lambda bi: (bi, 0, 0, 0)),
        out_shape=jax.ShapeDtypeStruct((bsz, slots, heads, V7X_LANES), jnp.int32),
        compiler_params=_params("parallel"),
        name="moba_select",
    )(q, kmean)
    return sel[..., 0]


def _moba_decode_kernel(pt_ref, sel_ref, q_ref, kn_ref, vn_ref, kc_hbm, vc_hbm, o_ref,
                        kbuf, vbuf, sems, *, layer, heads, top_k, pages_per_block, n_q):
    b = pl.program_id(0)
    h = pl.program_id(1)
    step = b * heads + h
    n_steps = pl.num_programs(0) * heads
    cur = lax.rem(step, 2)
    n_slots = n_q * top_k
    n_pages = n_slots * pages_per_block
    rows, hd = q_ref.shape[1:]
    page = kn_ref.shape[1]
    scale = hd ** -0.5
    per_query = top_k * pages_per_block

    def page_copies(bb, hh, half, i):
        slot, pg = divmod(i, pages_per_block)
        t, j = divmod(i, per_query)
        blk = sel_ref[(bb * n_slots + slot) * heads + hh]
        phys = pt_ref[bb, blk * pages_per_block + pg]
        dst = pl.ds(j * page, page)
        return (pltpu.make_async_copy(kc_hbm.at[layer, phys, :, hh, :], kbuf.at[half, t, dst], sems.at[0, half, i]),
                pltpu.make_async_copy(vc_hbm.at[layer, phys, :, hh, :], vbuf.at[half, t, dst], sems.at[1, half, i]))

    def start_gather(bb, hh, half):
        for i in range(n_pages):
            for c in page_copies(bb, hh, half, i):
                c.start()

    @pl.when(step == 0)
    def _():
        start_gather(b, h, cur)

    @pl.when(step + 1 < n_steps)
    def _():
        nxt = step + 1
        start_gather(nxt // heads, lax.rem(nxt, heads), 1 - cur)

    for i in range(n_pages):
        for c in page_copies(b, h, cur, i):
            c.wait()

    qb = q_ref[0].astype(BF16)
    lane = lax.broadcasted_iota(jnp.int32, (1, page), 1)
    rowi = lax.broadcasted_iota(jnp.int32, (rows, 1), 0)
    s_own = _dot_nt(qb, kn_ref[0].astype(BF16)) * scale
    v_own = vn_ref[0].astype(BF16)
    s_past = [_dot_nt(qb, kbuf[cur, t].astype(BF16)) * scale for t in range(n_q)]
    s_new = [jnp.where(lane <= t, s_own, NEG_INF) for t in range(n_q)]
    m = [jnp.maximum(jnp.max(sp, axis=1, keepdims=True), jnp.max(sn, axis=1, keepdims=True))
         for sp, sn in zip(s_past, s_new)]
    p_past = [jnp.exp(sp - mt) for sp, mt in zip(s_past, m)]
    p_new = [jnp.exp(sn - mt) for sn, mt in zip(s_new, m)]
    l = [jnp.sum(pp, axis=1, keepdims=True) + jnp.sum(pn, axis=1, keepdims=True)
         for pp, pn in zip(p_past, p_new)]
    acc = [_dot(pp.astype(BF16), vbuf[cur, t].astype(BF16)) + _dot(pn.astype(BF16), v_own)
           for t, (pp, pn) in enumerate(zip(p_past, p_new))]
    out = jnp.zeros((rows, hd), F32)
    for t in range(n_q):
        out = jnp.where(rowi == t, acc[t] / l[t], out)
    o_ref[0] = out


def _moba_decode(q, k_new, v_new, cache_k, cache_v, page_table, sel, layer, n_q):
    bsz, rows, d = q.shape
    _, _, page, heads, hd = cache_k.shape
    ppb = MOBA_BLOCK // page
    top_k = MOBA_TOP_K
    n_pages = n_q * top_k * ppb
    assert n_q <= rows and k_new.shape[1] == page
    q_spec = pl.BlockSpec((1, rows, hd), lambda bi, hi, pt, sl: (bi, 0, hi))
    new_spec = pl.BlockSpec((1, page, hd), lambda bi, hi, pt, sl: (bi, 0, hi))
    any_spec = pl.BlockSpec(memory_space=pl.ANY)
    grid_spec = pltpu.PrefetchScalarGridSpec(
        num_scalar_prefetch=2,
        grid=(bsz, heads),
        in_specs=[q_spec, new_spec, new_spec, any_spec, any_spec],
        out_specs=q_spec,
        scratch_shapes=[pltpu.VMEM((2, n_q, top_k * ppb * page, hd), F32),
                        pltpu.VMEM((2, n_q, top_k * ppb * page, hd), F32),
                        pltpu.SemaphoreType.DMA((2, 2, n_pages))],
    )
    return pl.pallas_call(
        functools.partial(_moba_decode_kernel, layer=layer, heads=heads, top_k=top_k,
                          pages_per_block=ppb, n_q=n_q),
        grid_spec=grid_spec,
        out_shape=jax.ShapeDtypeStruct((bsz, rows, d), F32),
        compiler_params=_params("arbitrary", "arbitrary"),
        name="moba_decode",
    )(page_table, sel.reshape(-1), q, k_new, v_new, cache_k, cache_v)


def kernel(x_prompt, x_sample, state_gla, cache_k, cache_v, page_table, norm_mix, norm_ffn, norm_final,
           gla_w_in, gla_w_gate, gla_b_gate, gla_head_norm, gla_w_out, moba_w_qkv, moba_w_o,
           ffn_w_gate_up, ffn_w_down):
    bp, seq, d = x_prompt.shape
    bd, sd, _ = x_sample.shape
    depth = norm_mix.shape[0]
    n_moba, n_pool, page, mh, mhd = cache_k.shape
    past_len = page_table.shape[1] * page
    assert past_len % MOBA_BLOCK == 0 and sd <= min(MOBA_BLOCK, 8) and GLA_CHUNK % sd == 0
    gh, gdk, gdv = state_gla.shape[2:]

    xp = x_prompt.reshape(bp * seq, d)
    xs = x_sample.reshape(bd * sd, d)
    rope_p = _rope_tables(seq, 0, seq, mhd)
    rope_s = _rope_tables(bd * sd, past_len, sd, mhd)
    kmean = _past_block_means(cache_k, page_table)
    pad_s = GLA_CHUNK - sd

    def pad_tokens(a, n):
        c = a.shape[1]
        return jnp.pad(a.reshape(bd, sd, c), ((0, 0), (0, n), (0, 0))).reshape(bd * (sd + n), c)

    gla_sp, gla_ss, kp_l, vp_l, ks_l, vs_l = [], [], [], [], [], []
    for i in range(depth):
        j = i // 2
        hp = _rmsnorm(xp, norm_mix[i], BF16)
        hs = _rmsnorm(xs, norm_mix[i], BF16)
        if i % 2 == 0:
            n_main = 2 * gh * gdk + 2 * gh * gdv
            proj_p, proj_s = _matmul(hp, hs, gla_w_in, j, n_main)
            g_p = _gla_gate(hp, gla_w_in, j, n_main, gla_w_gate[j], gla_b_gate[j])
            g_s = _gla_gate(hs, gla_w_in, j, n_main, gla_w_gate[j], gla_b_gate[j])
            op, sp = _gla_core(proj_p, g_p, gla_head_norm[j], None, j, (bp, gh, gdk, gdv), seq, seq)
            os_pad, ss = _gla_core(pad_tokens(proj_s, pad_s), pad_tokens(g_s, pad_s), gla_head_norm[j],
                                   state_gla, j, (bd, gh, gdk, gdv), sd + pad_s, sd)
            os_ = os_pad.reshape(bd, sd + pad_s, d)[:, :sd].reshape(bd * sd, d)
            xp, xs = _matmul(op, os_, gla_w_out, j, d, res=(xp, xs))
            gla_sp.append(sp)
            gla_ss.append(ss)
        else:
            rope = (rope_p, rope_s)
            qp, qs = _matmul(hp, hs, moba_w_qkv, j, d, 0, rope=rope)
            kp, ks = _matmul(hp, hs, moba_w_qkv, j, d, d, rope=rope)
            vp, vs = _matmul(hp, hs, moba_w_qkv, j, d, 2 * d)
            op = _moba_prompt(qp, kp, vp, bp, seq, mh)
            q3 = pad_tokens(qs, DECODE_Q_ROWS - sd).reshape(bd, DECODE_Q_ROWS, d)
            k3, v3 = (pad_tokens(a, page - sd).reshape(bd, page, d) for a in (ks, vs))
            sel = _moba_select(qs.reshape(bd, sd, mh, mhd), kmean, j, sd)
            o3 = _moba_decode(q3, k3, v3, cache_k, cache_v, page_table, sel, j, sd)
            xp, xs = _matmul(op, o3[:, :sd].reshape(bd * sd, d), moba_w_o, j, d, res=(xp, xs))
            kp_l.append(kp)
            vp_l.append(vp)
            ks_l.append(ks)
            vs_l.append(vs)
        xp, xs = _ffn(xp, xs, norm_ffn[i], ffn_w_gate_up, ffn_w_down, i)

    y_prompt = _rmsnorm(xp, norm_final, F32).reshape(bp, seq, d)
    y_sample = _rmsnorm(xs, norm_final, F32).reshape(bd, sd, d)
    def pages(rows_per_layer, lead):
        return jnp.stack(rows_per_layer).reshape(len(rows_per_layer), *lead, mh, mhd)

    return (y_prompt, y_sample, jnp.stack(gla_sp), jnp.stack(gla_ss),
            pages(kp_l, (bp * seq // page, page)), pages(vp_l, (bp * seq // page, page)),
            pages(ks_l, (bd, sd)), pages(vs_l, (bd, sd)))
```

```python
import functools

import jax
import jax.numpy as jnp
from jax import lax
from jax.experimental import pallas as pl
from jax.experimental.pallas import tpu as pltpu

NORM_EPS = 1e-6
GLA_HEADS = 4
GLA_GATE_RANK = 16
GLA_GATE_NORMALIZER = 16.0
GLA_CHUNK = 64
GLA_SUB = 16
GLA_CHUNKS_PER_STEP = 4
MOBA_HEADS = 16
MOBA_BLOCK = 256
MOBA_TOP_K = 3
ROPE_THETA = 500000.0
ROPE_FRACTION = 4
PAGE_SIZE = 128

V7X_LANES = 128
V7X_VMEM_LIMIT_BYTES = 56 * 1024 * 1024

F32 = jnp.float32
BF16 = jnp.bfloat16
NEG_INF = float("-inf")
LOG2_E = 1.4426950408889634


def _params(*sem):
    return pltpu.CompilerParams(dimension_semantics=sem, vmem_limit_bytes=V7X_VMEM_LIMIT_BYTES)


def _pick(n, candidates):
    for c in candidates:
        if n % c == 0:
            return c
    raise ValueError(f"no tile for {n} in {candidates}")


def _dot(a, b):
    return jnp.dot(a, b, preferred_element_type=F32)


def _dot_nt(a, b):
    return lax.dot_general(a, b, (((1,), (1,)), ((), ())), preferred_element_type=F32)


def _dot_tn(a, b):
    return lax.dot_general(a, b, (((0,), (0,)), ((), ())), preferred_element_type=F32)


def _split2(x):
    hi = x.astype(BF16)
    lo = (x - hi.astype(F32)).astype(BF16)
    return hi, lo


def _split3(x):
    hi = x.astype(BF16)
    r1 = x - hi.astype(F32)
    mid = r1.astype(BF16)
    lo = (r1 - mid.astype(F32)).astype(BF16)
    return hi, mid, lo


def _rmsnorm_kernel(x_ref, g_ref, o_ref):
    x = x_ref[...]
    ms = jnp.mean(x * x, axis=-1, keepdims=True)
    o_ref[...] = (x * lax.rsqrt(ms + NORM_EPS) * g_ref[...]).astype(o_ref.dtype)


def _rmsnorm(x, g, out_dtype):
    m, d = x.shape
    tm = _pick(m, (512, 256, 128, 64, 32, 16))
    return pl.pallas_call(
        _rmsnorm_kernel,
        grid=(m // tm,),
        in_specs=[pl.BlockSpec((tm, d), lambda i: (i, 0)),
                  pl.BlockSpec((1, d), lambda i: (0, 0))],
        out_specs=pl.BlockSpec((tm, d), lambda i: (i, 0)),
        out_shape=jax.ShapeDtypeStruct((m, d), out_dtype),
        compiler_params=_params("parallel"),
        name="rmsnorm",
    )(x, g.reshape(1, d))


def _rope(acc, cos_ref, sin_ref, rot_half):
    tn = acc.shape[1]
    reps = tn // V7X_LANES
    cos = jnp.concatenate([cos_ref[...]] * reps, axis=1)
    sin = jnp.concatenate([sin_ref[...]] * reps, axis=1)
    lane = lax.broadcasted_iota(jnp.int32, acc.shape, 1) % V7X_LANES
    partner = jnp.where(lane < rot_half,
                        pltpu.roll(acc, tn - rot_half, 1),
                        pltpu.roll(acc, rot_half, 1))
    return acc * cos + partner * sin


def _mm_kernel(*refs, kind, rot_half):
    n_extra = {"plain": 0, "res": 1, "rope": 2}[kind]
    x_ref, w_ref = refs[:2]
    extras = refs[2:2 + n_extra]
    xd_ref = refs[2 + n_extra]
    extras_d = refs[3 + n_extra:3 + 2 * n_extra]
    o_ref, od_ref, wb_ref = refs[3 + 2 * n_extra:]

    def epilogue(acc, ex):
        if kind == "res":
            return ex[0][...] + acc
        if kind == "rope":
            return _rope(acc, ex[0], ex[1], rot_half)
        return acc

    @pl.when(pl.program_id(1) == 0)
    def _():
        wb_ref[...] = w_ref[...].astype(BF16)
        od_ref[...] = epilogue(_dot(xd_ref[...].astype(BF16), wb_ref[...]), extras_d).astype(od_ref.dtype)

    o_ref[...] = epilogue(_dot(x_ref[...].astype(BF16), wb_ref[...]), extras).astype(o_ref.dtype)


def _mm_swiglu_kernel(x_ref, wg_ref, wu_ref, xd_ref, o_ref, od_ref, wgb_ref, wub_ref):
    def gated(x):
        g = _dot(x, wgb_ref[...])
        u = _dot(x, wub_ref[...])
        return g * jax.nn.sigmoid(g) * u

    @pl.when(pl.program_id(1) == 0)
    def _():
        wgb_ref[...] = wg_ref[...].astype(BF16)
        wub_ref[...] = wu_ref[...].astype(BF16)
        od_ref[...] = gated(xd_ref[...]).astype(od_ref.dtype)

    o_ref[...] = gated(x_ref[...]).astype(o_ref.dtype)


def _mm_tiles(m, k, n):
    tm = _pick(m, (1024, 512, 256, 128, 64, 32, 16))
    tn = _pick(n, (1024, 512, 256, 128))
    if k > 4096:
        tm, tn = min(tm, 512), min(tn, 512)
    return tm, tn


def _matmul(x, xd, w, layer, n, col0=0, out_dtype=F32, res=None, rope=None):
    m, k = x.shape
    md = xd.shape[0]
    tm, tn = _mm_tiles(m, k, n)
    assert col0 % tn == 0 and w.shape[1] == k and xd.shape[1] == k
    cb0 = col0 // tn
    x_spec = pl.BlockSpec((tm, k), lambda j, i: (i, 0))
    xd_spec = pl.BlockSpec((md, k), lambda j, i: (0, 0))
    w_spec = pl.BlockSpec((None, k, tn), lambda j, i: (layer, 0, cb0 + j))
    o_spec = pl.BlockSpec((tm, tn), lambda j, i: (i, j))
    od_spec = pl.BlockSpec((md, tn), lambda j, i: (0, j))
    kind, rot_half, ex, ex_specs, exd, exd_specs = "plain", 0, (), [], (), []
    if res is not None:
        kind, ex, ex_specs, exd, exd_specs = "res", (res[0],), [o_spec], (res[1],), [od_spec]
    elif rope is not None:
        (cos, sin, rot_half), (cos_d, sin_d, _) = rope
        assert cos.shape[0] % tm == 0 and cos_d.shape[0] == md
        period = cos.shape[0] // tm
        kind = "rope"
        ex, ex_specs = (cos, sin), [pl.BlockSpec((tm, V7X_LANES), lambda j, i: (i % period, 0))] * 2
        exd, exd_specs = (cos_d, sin_d), [pl.BlockSpec((md, V7X_LANES), lambda j, i: (0, 0))] * 2
    return pl.pallas_call(
        functools.partial(_mm_kernel, kind=kind, rot_half=rot_half),
        grid=(n // tn, m // tm),
        in_specs=[x_spec, w_spec, *ex_specs, xd_spec, *exd_specs],
        out_specs=[o_spec, od_spec],
        out_shape=[jax.ShapeDtypeStruct((m, n), out_dtype), jax.ShapeDtypeStruct((md, n), out_dtype)],
        scratch_shapes=[pltpu.VMEM((k, tn), BF16)],
        compiler_params=_params("arbitrary", "arbitrary"),
        name="matmul",
    )(x, w, *ex, xd, *exd)


def _swiglu_up(x, xd, w_gate_up, layer):
    m, k = x.shape
    md = xd.shape[0]
    f = w_gate_up.shape[2] // 2
    tm, tn = _mm_tiles(m, k, f)
    nb = f // tn
    return pl.pallas_call(
        _mm_swiglu_kernel,
        grid=(nb, m // tm),
        in_specs=[pl.BlockSpec((tm, k), lambda j, i: (i, 0)),
                  pl.BlockSpec((None, k, tn), lambda j, i: (layer, 0, j)),
                  pl.BlockSpec((None, k, tn), lambda j, i: (layer, 0, nb + j)),
                  pl.BlockSpec((md, k), lambda j, i: (0, 0))],
        out_specs=[pl.BlockSpec((tm, tn), lambda j, i: (i, j)),
                   pl.BlockSpec((md, tn), lambda j, i: (0, j))],
        out_shape=[jax.ShapeDtypeStruct((m, f), BF16), jax.ShapeDtypeStruct((md, f), BF16)],
        scratch_shapes=[pltpu.VMEM((k, tn), BF16), pltpu.VMEM((k, tn), BF16)],
        compiler_params=_params("arbitrary", "arbitrary"),
        name="swiglu_up",
    )(x, w_gate_up, w_gate_up, xd)


def _ffn(x, xd, g, w_gate_up, w_down, layer):
    h, hdec = _rmsnorm(x, g, BF16), _rmsnorm(xd, g, BF16)
    hid, hid_d = _swiglu_up(h, hdec, w_gate_up, layer)
    return _matmul(hid, hid_d, w_down, layer, w_down.shape[2], res=(x, xd))


def _rope_table_kernel(inv_ref, cos_ref, sin_ref, *, base, period, rot_half):
    rows = cos_ref.shape[0]
    row = lax.broadcasted_iota(jnp.int32, (rows, V7X_LANES), 0) + pl.program_id(0) * rows
    lane = lax.broadcasted_iota(jnp.int32, (rows, V7X_LANES), 1)
    pos = (base + (row & (period - 1))).astype(F32)
    ang = pos * inv_ref[...]
    c, s = jnp.cos(ang), jnp.sin(ang)
    cos_ref[...] = jnp.where(lane < 2 * rot_half, c, 1.0)
    sin_ref[...] = jnp.where(lane < rot_half, -s, jnp.where(lane < 2 * rot_half, s, 0.0))


def _rope_tables(rows, base, period, head_dim):
    assert period & (period - 1) == 0 and head_dim == V7X_LANES
    rot_half = head_dim // ROPE_FRACTION // 2
    inv = ROPE_THETA ** (-jnp.arange(rot_half, dtype=F32) / rot_half)
    inv = jnp.concatenate([inv, inv, jnp.zeros((head_dim - 2 * rot_half,), F32)]).reshape(1, head_dim)
    tr = _pick(rows, (256, 128, 64, 32, 16, 8))
    spec = pl.BlockSpec((tr, head_dim), lambda i: (i, 0))
    cos, sin = pl.pallas_call(
        functools.partial(_rope_table_kernel, base=base, period=period, rot_half=rot_half),
        grid=(rows // tr,),
        in_specs=[pl.BlockSpec((1, head_dim), lambda i: (0, 0))],
        out_specs=[spec, spec],
        out_shape=[jax.ShapeDtypeStruct((rows, head_dim), F32)] * 2,
        compiler_params=_params("parallel"),
        name="rope_tables",
    )(inv)
    return cos, sin, rot_half


def _gla_gate_kernel(h_ref, wlr_ref, wg_ref, b_ref, o_ref):
    rank = wg_ref.shape[0]
    glr = _dot(h_ref[...], wlr_ref[:, :rank].astype(BF16))
    z = _dot(glr.astype(BF16), wg_ref[...].astype(BF16)) + b_ref[...]
    o_ref[...] = (jnp.minimum(z, 0.0) - jnp.log1p(jnp.exp(-jnp.abs(z)))) * (1.0 / GLA_GATE_NORMALIZER)


def _gla_gate(h, w_in, layer, col0, w_gate, b_gate):
    m, d = h.shape
    r, kd = w_gate.shape
    assert col0 % V7X_LANES == 0 and col0 + r == w_in.shape[2] and r <= V7X_LANES
    tm = _pick(m, (512, 256, 128, 64, 32, 16))
    return pl.pallas_call(
        _gla_gate_kernel,
        grid=(m // tm,),
        in_specs=[pl.BlockSpec((tm, d), lambda i: (i, 0)),
                  pl.BlockSpec((None, d, V7X_LANES), lambda i: (layer, 0, col0 // V7X_LANES)),
                  pl.BlockSpec((r, kd), lambda i: (0, 0)),
                  pl.BlockSpec((1, kd), lambda i: (0, 0))],
        out_specs=pl.BlockSpec((tm, kd), lambda i: (i, 0)),
        out_shape=jax.ShapeDtypeStruct((m, kd), F32),
        compiler_params=_params("parallel"),
        name="gla_gate",
    )(h, w_in, w_gate, b_gate.reshape(1, kd))


def _block_mean_stream(pt_ref, cache_hbm, km_ref, pbuf, sems, *, layer, step, n_steps, pages_per_block):
    per_step, page = pbuf.shape[1:3]
    table_pages = pt_ref.shape[1]
    cur = lax.rem(step, 2)

    def page_copy(s, half, a):
        flat = s * per_step + a
        phys = pt_ref[flat // table_pages, lax.rem(flat, table_pages)]
        return pltpu.make_async_copy(cache_hbm.at[layer, phys], pbuf.at[half, a], sems.at[half, a])

    def start_fetch(s, half):
        for a in range(per_step):
            page_copy(s, half, a).start()

    @pl.when(step == 0)
    def _():
        start_fetch(step, cur)

    @pl.when(step + 1 < n_steps)
    def _():
        start_fetch(step + 1, 1 - cur)

    for a in range(per_step):
        page_copy(step, cur, a).wait()

    for a0 in range(0, per_step, pages_per_block):
        total = functools.reduce(jnp.add, [jnp.sum(pbuf[cur, a], axis=0)
                                           for a in range(a0, a0 + pages_per_block)])
        flat = step * per_step + a0
        block = lax.rem(flat, table_pages) // pages_per_block
        km_ref[flat // table_pages, block] = total * (1.0 / (pages_per_block * page))


def _gla_kernel(*refs, chunk, sub, n_valid, has_state, side):
    refs = list(refs)
    pt_ref = refs.pop(0) if side else None
    q_ref, k_ref, v_ref, g_ref, r_ref, hn_ref = refs[:6]
    del refs[:6]
    s0_ref = refs.pop(0) if has_state else None
    cache_hbm = refs.pop(0) if side else None
    o_ref, s_ref = refs[:2]
    del refs[:2]
    km_ref = refs.pop(0) if side else None
    st_ref = refs.pop(0)
    ci = pl.program_id(2)
    span, dk = q_ref.shape
    half = sub // 2

    if side:
        pbuf, sems = refs
        step = (pl.program_id(0) * pl.num_programs(1) + pl.program_id(1)) * pl.num_programs(2) + ci
        n_steps = pl.num_programs(0) * pl.num_programs(1) * pl.num_programs(2)
        _block_mean_stream(pt_ref, cache_hbm, km_ref, pbuf, sems, layer=side[0], step=step,
                           n_steps=n_steps, pages_per_block=side[1])

    @pl.when(ci == 0)
    def _():
        if s0_ref is None:
            st_ref[...] = jnp.zeros(st_ref.shape, st_ref.dtype)
        else:
            st_ref[...] = s0_ref[0, 0].T

    row = lax.broadcasted_iota(jnp.int32, (chunk, chunk), 0)
    col = lax.broadcasted_iota(jnp.int32, (chunk, chunk), 1)
    tri = jnp.where(col <= row, 1.0, 0.0).astype(BF16)
    krow = lax.broadcasted_iota(jnp.int32, (chunk, 1), 0)
    hrow = lax.broadcasted_iota(jnp.int32, (half, 1), 0)
    hcol = lax.broadcasted_iota(jnp.int32, (half, chunk), 1)
    hn = hn_ref[...]

    for c0 in range(0, span, chunk):
        q = q_ref[c0:c0 + chunk, :] * (dk ** -0.5)
        k = k_ref[c0:c0 + chunk, :]
        v = v_ref[c0:c0 + chunk, :].astype(BF16)
        tok = krow + (ci * span + c0)
        g = jnp.where(tok < n_valid, g_ref[c0:c0 + chunk, :], 0.0) * LOG2_E
        g_hi, g_mid, g_lo = _split3(g)
        b = _dot(tri, g_hi) + _dot(tri, g_mid) + _dot(tri, g_lo)
        b_last = b[chunk - 1:chunk, :]

        st = st_ref[...]
        o = _dot_nt((q * jnp.exp2(b)).astype(BF16), st.astype(BF16))

        blocks = []
        for i in range(chunk // sub):
            lo = i * sub
            bi, qi, ki = b[lo:lo + sub], q[lo:lo + sub], k[lo:lo + sub]
            for hh in range(2):
                bh, qh = bi[hh * half:(hh + 1) * half], qi[hh * half:(hh + 1) * half]
                a = jnp.zeros((half, chunk), F32)
                for jj in range((hh + 1) * half):
                    diff = bh - bi[jj:jj + 1]
                    if jj >= hh * half:
                        diff = jnp.where(hrow >= jj - hh * half, diff, NEG_INF)
                    pair = jnp.sum(qh * jnp.exp2(diff) * ki[jj:jj + 1], axis=-1, keepdims=True)
                    a = jnp.where(hcol == lo + jj, pair, a)
                blocks.append(a)
            if i > 0:
                ref_b = bi[0:1]
                qs = (qi * jnp.exp2(bi - ref_b)).astype(BF16)
                ks = (k * jnp.exp2(jnp.where(krow < lo, ref_b - b, NEG_INF))).astype(BF16)
                far = _dot_nt(qs, ks)
                blocks[-2] = blocks[-2] + far[:half]
                blocks[-1] = blocks[-1] + far[half:]
        attn = jnp.concatenate(blocks, axis=0)
        o = o + _dot(attn.astype(BF16), v)

        k_dec = (k * jnp.exp2(b_last - b)).astype(BF16)
        st_ref[...] = st * jnp.exp2(b_last) + _dot_tn(v, k_dec)

        ms = jnp.mean(o * o, axis=-1, keepdims=True)
        y = o * lax.rsqrt(ms + NORM_EPS) * hn
        r = r_ref[c0:c0 + chunk, :]
        o_ref[c0:c0 + chunk, :] = (y * (r * jax.nn.sigmoid(r))).astype(o_ref.dtype)

    @pl.when(ci == pl.num_programs(2) - 1)
    def _():
        s_ref[0, 0] = st_ref[...].T


def _gla_core(proj, g, head_norm, state, layer, dims, seq, n_valid, block_means_of=None):
    m = proj.shape[0]
    bsz, heads, dk, dv = dims
    chunk = GLA_CHUNK
    assert seq % chunk == 0 and chunk % GLA_SUB == 0 and GLA_SUB % 16 == 0 and (2 * heads * dk) % dv == 0
    span = chunk * _pick(seq // chunk, (GLA_CHUNKS_PER_STEP, 2, 1))
    nc = seq // span
    kb0 = heads
    vb0 = 2 * heads * dk // dv
    rb0 = vb0 + heads

    def rows(bi, hi, ci):
        return bi * nc + ci

    in_specs = [pl.BlockSpec((span, dk), lambda bi, hi, ci, *_: (rows(bi, hi, ci), hi)),
                pl.BlockSpec((span, dk), lambda bi, hi, ci, *_: (rows(bi, hi, ci), kb0 + hi)),
                pl.BlockSpec((span, dv), lambda bi, hi, ci, *_: (rows(bi, hi, ci), vb0 + hi)),
                pl.BlockSpec((span, dk), lambda bi, hi, ci, *_: (rows(bi, hi, ci), hi)),
                pl.BlockSpec((span, dv), lambda bi, hi, ci, *_: (rows(bi, hi, ci), rb0 + hi)),
                pl.BlockSpec((1, dv), lambda bi, hi, ci, *_: (0, 0))]
    operands = [proj, proj, proj, g, proj, head_norm.reshape(1, dv)]
    out_specs = [pl.BlockSpec((span, dv), lambda bi, hi, ci, *_: (rows(bi, hi, ci), hi)),
                 pl.BlockSpec((1, 1, dk, dv), lambda bi, hi, ci, *_: (bi, hi, 0, 0))]
    out_shape = [jax.ShapeDtypeStruct((m, heads * dv), BF16),
                 jax.ShapeDtypeStruct((bsz, heads, dk, dv), F32)]
    scratch = [pltpu.VMEM((dv, dk), F32)]
    prefetch = []
    if state is not None:
        in_specs.append(pl.BlockSpec((None, 1, 1, dk, dv), lambda bi, hi, ci, *_: (layer, bi, hi, 0, 0)))
        operands.append(state)
    side = None
    if block_means_of is not None:
        cache, page_table = block_means_of
        _, _, page, c_heads, c_hd = cache.shape
        c_bsz, table_pages = page_table.shape
        ppb = MOBA_BLOCK // page
        n_steps = bsz * heads * nc
        per_step = c_bsz * table_pages // n_steps
        assert per_step * n_steps == c_bsz * table_pages and per_step % ppb == 0 and table_pages % per_step == 0
        side = (layer, ppb)
        prefetch = [page_table]
        in_specs.append(pl.BlockSpec(memory_space=pl.ANY))
        operands.append(cache)
        km_shape = (c_bsz, table_pages // ppb, c_heads, c_hd)
        out_specs.append(pl.BlockSpec(km_shape, lambda bi, hi, ci, *_: (0, 0, 0, 0)))
        out_shape.append(jax.ShapeDtypeStruct(km_shape, F32))
        scratch += [pltpu.VMEM((2, per_step, page, c_heads, c_hd), F32), pltpu.SemaphoreType.DMA((2, per_step))]
    grid_spec = pltpu.PrefetchScalarGridSpec(
        num_scalar_prefetch=len(prefetch), grid=(bsz, heads, nc),
        in_specs=in_specs, out_specs=out_specs, scratch_shapes=scratch)
    return pl.pallas_call(
        functools.partial(_gla_kernel, chunk=chunk, sub=GLA_SUB, n_valid=n_valid,
                          has_state=state is not None, side=side),
        grid_spec=grid_spec,
        out_shape=out_shape,
        compiler_params=_params("arbitrary", "arbitrary", "arbitrary"),
        name="gla_core",
    )(*prefetch, *operands)


def _moba_prompt_kernel(q_ref, k_ref, v_ref, o_ref, kb_ref, vt_ref, km_ref, *, blk, top_k):
    seq, hd = q_ref.shape
    nb = seq // blk
    score_scale = (hd ** -0.5) * LOG2_E

    for n in range(nb):
        kblk = k_ref[n * blk:(n + 1) * blk, :]
        km_ref[n:n + 1, :] = jnp.mean(kblk, axis=0, keepdims=True)
        kb_ref[n * blk:(n + 1) * blk, :] = kblk.astype(BF16)
        vt_ref[:, n * blk:(n + 1) * blk] = v_ref[n * blk:(n + 1) * blk, :].T.astype(BF16)
    km_hi, km_lo = _split2(km_ref[...])

    brow = lax.broadcasted_iota(jnp.int32, (nb, blk), 0)
    key = lax.broadcasted_iota(jnp.int32, (blk, blk), 0)
    qry = lax.broadcasted_iota(jnp.int32, (blk, blk), 1)
    causal = jnp.where(key <= qry, 0.0, NEG_INF)

    def scores(i):
        q = q_ref[i * blk:(i + 1) * blk, :]
        s = _dot_nt(kb_ref[0:(i + 1) * blk, :], q.astype(BF16)) * score_scale
        pieces = []
        if i > 0:
            q_hi, q_lo = _split2(q)
            gate = _dot_nt(km_hi, q_hi) + _dot_nt(km_lo, q_hi) + _dot_nt(km_hi, q_lo)
            past = brow < i
            gate = jnp.where(past, gate, NEG_INF)
            rank = jnp.zeros((nb, blk), jnp.int32)
            for n in range(i):
                gn = gate[n:n + 1, :]
                rank = rank + jnp.where(gn > gate, 1, jnp.where(gn == gate, jnp.where(brow > n, 1, 0), 0))
            bias = jnp.where(past, jnp.where(rank < top_k, 0.0, NEG_INF), NEG_INF)
            pieces = [s[n * blk:(n + 1) * blk] + bias[n:n + 1] for n in range(i)]
        return pieces + [s[i * blk:(i + 1) * blk] + causal]

    def softmax(pieces):
        m = functools.reduce(jnp.maximum, [jnp.max(p, axis=0, keepdims=True) for p in pieces])
        probs = [jnp.exp2(p - m) for p in pieces]
        l = functools.reduce(jnp.add, [jnp.sum(p, axis=0, keepdims=True) for p in probs])
        return jnp.concatenate([p.astype(BF16) for p in probs], axis=0), l

    def attend(i, probs, l):
        acc = _dot(vt_ref[:, 0:(i + 1) * blk], probs)
        o_ref[i * blk:(i + 1) * blk, :] = (acc / l).T.astype(o_ref.dtype)

    pending = scores(0)
    for i in range(nb):
        upcoming = scores(i + 1) if i + 1 < nb else None
        probs, l = softmax(pending)
        attend(i, probs, l)
        pending = upcoming


def _moba_prompt(q, k, v, bsz, seq, heads):
    m, d = q.shape
    hd = d // heads
    blk = MOBA_BLOCK
    assert seq % blk == 0 and hd == V7X_LANES
    nb = seq // blk
    spec = pl.BlockSpec((seq, hd), lambda bi, hi: (bi, hi))
    return pl.pallas_call(
        functools.partial(_moba_prompt_kernel, blk=blk, top_k=MOBA_TOP_K),
        grid=(bsz, heads),
        in_specs=[spec, spec, spec],
        out_specs=spec,
        out_shape=jax.ShapeDtypeStruct((m, d), BF16),
        scratch_shapes=[pltpu.VMEM((seq, hd), BF16),
                        pltpu.VMEM((hd, seq), BF16),
                        pltpu.VMEM((nb, hd), F32)],
        compiler_params=_params("parallel", "parallel"),
        name="moba_prompt",
    )(q, k, v)


DECODE_Q_ROWS = 16


def _moba_select_kernel(q_ref, km_ref, o_ref, *, top_k, n_q):
    km = km_ref[0]
    nbp, heads, _ = km.shape
    bidx = lax.broadcasted_iota(jnp.int32, (nbp, heads, 1), 0)
    for t in range(n_q):
        gate = jnp.sum(km * q_ref[0, t][None], axis=-1, keepdims=True)
        for r in range(top_k):
            best = jnp.max(gate, axis=0, keepdims=True)
            idx = jnp.min(jnp.where(gate == best, bidx, nbp), axis=0, keepdims=True)
            o_ref[0, t * top_k + r] = jnp.broadcast_to(idx[0], o_ref.shape[2:])
            gate = jnp.where(bidx == idx, NEG_INF, gate)


def _moba_select(q, kmean, n_q):
    bsz, _, heads, hd = q.shape
    nbp = kmean.shape[1]
    slots = n_q * MOBA_TOP_K
    sel = pl.pallas_call(
        functools.partial(_moba_select_kernel, top_k=MOBA_TOP_K, n_q=n_q),
        grid=(bsz,),
        in_specs=[pl.BlockSpec((1, n_q, heads, hd), lambda bi: (bi, 0, 0, 0)),
                  pl.BlockSpec((1, nbp, heads, hd), lambda bi: (bi, 0, 0, 0))],
        out_specs=pl.BlockSpec((1, slots, heads, V7X_LANES), lambda bi: (bi, 0, 0, 0)),
        out_shape=jax.ShapeDtypeStruct((bsz, slots, heads, V7X_LANES), jnp.int32),
        compiler_params=_params("parallel"),
        name="moba_select",
    )(q, kmean)
    return sel[..., 0]


def _moba_decode_kernel(pt_ref, sel_ref, q_ref, kn_ref, vn_ref, kc_hbm, vc_hbm, o_ref,
                        kbuf, vbuf, sems, *, layer, heads, top_k, pages_per_block, n_q):
    b = pl.program_id(0)
    h = pl.program_id(1)
    step = b * heads + h
    n_steps = pl.num_programs(0) * heads
    cur = lax.rem(step, 2)
    n_slots = n_q * top_k
    n_pages = n_slots * pages_per_block
    rows, hd = q_ref.shape[1:]
    page = kn_ref.shape[1]
    scale = hd ** -0.5
    per_query = top_k * pages_per_block

    def page_copies(bb, hh, half, i):
        slot, pg = divmod(i, pages_per_block)
        t, j = divmod(i, per_query)
        blk = sel_ref[(bb * n_slots + slot) * heads + hh]
        phys = pt_ref[bb, blk * pages_per_block + pg]
        dst = pl.ds(j * page, page)
        return (pltpu.make_async_copy(kc_hbm.at[layer, phys, :, hh, :], kbuf.at[half, t, dst], sems.at[0, half, i]),
                pltpu.make_async_copy(vc_hbm.at[layer, phys, :, hh, :], vbuf.at[half, t, dst], sems.at[1, half, i]))

    def start_gather(bb, hh, half):
        for i in range(n_pages):
            for c in page_copies(bb, hh, half, i):
                c.start()

    @pl.when(step == 0)
    def _():
        start_gather(b, h, cur)

    @pl.when(step + 1 < n_steps)
    def _():
        nxt = step + 1
        start_gather(nxt // heads, lax.rem(nxt, heads), 1 - cur)

    for i in range(n_pages):
        for c in page_copies(b, h, cur, i):
            c.wait()

    qb = q_ref[0].astype(BF16)
    lane = lax.broadcasted_iota(jnp.int32, (1, page), 1)
    rowi = lax.broadcasted_iota(jnp.int32, (rows, 1), 0)
    s_own = _dot_nt(qb, kn_ref[0].astype(BF16)) * scale
    v_own = vn_ref[0].astype(BF16)
    s_past = [_dot_nt(qb, kbuf[cur, t].astype(BF16)) * scale for t in range(n_q)]
    s_new = [jnp.where(lane <= t, s_own, NEG_INF) for t in range(n_q)]
    m = [jnp.maximum(jnp.max(sp, axis=1, keepdims=True), jnp.max(sn, axis=1, keepdims=True))
         for sp, sn in zip(s_past, s_new)]
    p_past = [jnp.exp(sp - mt) for sp, mt in zip(s_past, m)]
    p_new = [jnp.exp(sn - mt) for sn, mt in zip(s_new, m)]
    l = [jnp.sum(pp, axis=1, keepdims=True) + jnp.sum(pn, axis=1, keepdims=True)
         for pp, pn in zip(p_past, p_new)]
    acc = [_dot(pp.astype(BF16), vbuf[cur, t].astype(BF16)) + _dot(pn.astype(BF16), v_own)
           for t, (pp, pn) in enumerate(zip(p_past, p_new))]
    out = jnp.zeros((rows, hd), F32)
    for t in range(n_q):
        out = jnp.where(rowi == t, acc[t] / l[t], out)
    o_ref[0] = out


def _moba_decode(q, k_new, v_new, cache_k, cache_v, page_table, sel, layer, n_q):
    bsz, rows, d = q.shape
    _, _, page, heads, hd = cache_k.shape
    ppb = MOBA_BLOCK // page
    top_k = MOBA_TOP_K
    n_pages = n_q * top_k * ppb
    assert n_q <= rows and k_new.shape[1] == page
    q_spec = pl.BlockSpec((1, rows, hd), lambda bi, hi, pt, sl: (bi, 0, hi))
    new_spec = pl.BlockSpec((1, page, hd), lambda bi, hi, pt, sl: (bi, 0, hi))
    any_spec = pl.BlockSpec(memory_space=pl.ANY)
    grid_spec = pltpu.PrefetchScalarGridSpec(
        num_scalar_prefetch=2,
        grid=(bsz, heads),
        in_specs=[q_spec, new_spec, new_spec, any_spec, any_spec],
        out_specs=q_spec,
        scratch_shapes=[pltpu.VMEM((2, n_q, top_k * ppb * page, hd), F32),
                        pltpu.VMEM((2, n_q, top_k * ppb * page, hd), F32),
                        pltpu.SemaphoreType.DMA((2, 2, n_pages))],
    )
    return pl.pallas_call(
        functools.partial(_moba_decode_kernel, layer=layer, heads=heads, top_k=top_k,
                          pages_per_block=ppb, n_q=n_q),
        grid_spec=grid_spec,
        out_shape=jax.ShapeDtypeStruct((bsz, rows, d), F32),
        compiler_params=_params("arbitrary", "arbitrary"),
        name="moba_decode",
    )(page_table, sel.reshape(-1), q, k_new, v_new, cache_k, cache_v)


def kernel(x_prompt, x_sample, state_gla, cache_k, cache_v, page_table, norm_mix, norm_ffn, norm_final,
           gla_w_in, gla_w_gate, gla_b_gate, gla_head_norm, gla_w_out, moba_w_qkv, moba_w_o,
           ffn_w_gate_up, ffn_w_down):
    bp, seq, d = x_prompt.shape
    bd, sd, _ = x_sample.shape
    depth = norm_mix.shape[0]
    n_moba, n_pool, page, mh, mhd = cache_k.shape
    past_len = page_table.shape[1] * page
    assert past_len % MOBA_BLOCK == 0 and sd <= min(MOBA_BLOCK, 8) and GLA_CHUNK % sd == 0
    gh, gdk, gdv = state_gla.shape[2:]

    xp = x_prompt.reshape(bp * seq, d)
    xs = x_sample.reshape(bd * sd, d)
    rope_p = _rope_tables(seq, 0, seq, mhd)
    rope_s = _rope_tables(bd * sd, past_len, sd, mhd)
    kmean = {}
    pad_s = GLA_CHUNK - sd

    def pad_tokens(a, n):
        c = a.shape[1]
        return jnp.pad(a.reshape(bd, sd, c), ((0, 0), (0, n), (0, 0))).reshape(bd * (sd + n), c)

    gla_sp, gla_ss, kp_l, vp_l, ks_l, vs_l = [], [], [], [], [], []
    for i in range(depth):
        j = i // 2
        hp = _rmsnorm(xp, norm_mix[i], BF16)
        hs = _rmsnorm(xs, norm_mix[i], BF16)
        if i % 2 == 0:
            n_main = 2 * gh * gdk + 2 * gh * gdv
            proj_p, proj_s = _matmul(hp, hs, gla_w_in, j, n_main)
            g_p = _gla_gate(hp, gla_w_in, j, n_main, gla_w_gate[j], gla_b_gate[j])
            g_s = _gla_gate(hs, gla_w_in, j, n_main, gla_w_gate[j], gla_b_gate[j])
            assert i + 1 < depth and j < n_moba
            op, sp, kmean[j] = _gla_core(proj_p, g_p, gla_head_norm[j], None, j, (bp, gh, gdk, gdv), seq, seq,
                                         block_means_of=(cache_k, page_table))
            os_pad, ss = _gla_core(pad_tokens(proj_s, pad_s), pad_tokens(g_s, pad_s), gla_head_norm[j],
                                   state_gla, j, (bd, gh, gdk, gdv), sd + pad_s, sd)
            os_ = os_pad.reshape(bd, sd + pad_s, d)[:, :sd].reshape(bd * sd, d)
            xp, xs = _matmul(op, os_, gla_w_out, j, d, res=(xp, xs))
            gla_sp.append(sp)
            gla_ss.append(ss)
        else:
            rope = (rope_p, rope_s)
            qp, qs = _matmul(hp, hs, moba_w_qkv, j, d, 0, rope=rope)
            kp, ks = _matmul(hp, hs, moba_w_qkv, j, d, d, rope=rope)
            vp, vs = _matmul(hp, hs, moba_w_qkv, j, d, 2 * d)
            op = _moba_prompt(qp, kp, vp, bp, seq, mh)
            q3 = pad_tokens(qs, DECODE_Q_ROWS - sd).reshape(bd, DECODE_Q_ROWS, d)
            k3, v3 = (pad_tokens(a, page - sd).reshape(bd, page, d) for a in (ks, vs))
            sel = _moba_select(qs.reshape(bd, sd, mh, mhd), kmean[j], sd)
            o3 = _moba_decode(q3, k3, v3, cache_k, cache_v, page_table, sel, j, sd)
            xp, xs = _matmul(op, o3[:, :sd].reshape(bd * sd, d), moba_w_o, j, d, res=(xp, xs))
            kp_l.append(kp)
            vp_l.append(vp)
            ks_l.append(ks)
            vs_l.append(vs)
        xp, xs = _ffn(xp, xs, norm_ffn[i], ffn_w_gate_up, ffn_w_down, i)

    y_prompt = _rmsnorm(xp, norm_final, F32).reshape(bp, seq, d)
    y_sample = _rmsnorm(xs, norm_final, F32).reshape(bd, sd, d)
    def pages(rows_per_layer, lead):
        return jnp.stack(rows_per_layer).reshape(len(rows_per_layer), *lead, mh, mhd)

    return (y_prompt, y_sample, jnp.stack(gla_sp), jnp.stack(gla_ss),
            pages(kp_l, (bp * seq // page, page)), pages(vp_l, (bp * seq // page, page)),
            pages(ks_l, (bd, sd)), pages(vs_l, (bd, sd)))
```

```python
import functools

import jax
import jax.numpy as jnp
from jax import lax
from jax.experimental import pallas as pl
from jax.experimental.pallas import tpu as pltpu

NORM_EPS = 1e-6
GLA_HEADS = 4
GLA_GATE_RANK = 16
GLA_GATE_NORMALIZER = 16.0
GLA_CHUNK = 64
GLA_SUB = 16
GLA_CHUNKS_PER_STEP = 4
MOBA_HEADS = 16
MOBA_BLOCK = 256
MOBA_TOP_K = 3
ROPE_THETA = 500000.0
ROPE_FRACTION = 4
PAGE_SIZE = 128

V7X_LANES = 128
V7X_VMEM_LIMIT_BYTES = 56 * 1024 * 1024

F32 = jnp.float32
BF16 = jnp.bfloat16
NEG_INF = float("-inf")
LOG2_E = 1.4426950408889634


def _params(*sem):
    return pltpu.CompilerParams(dimension_semantics=sem, vmem_limit_bytes=V7X_VMEM_LIMIT_BYTES)


def _pick(n, candidates):
    for c in candidates:
        if n % c == 0:
            return c
    raise ValueError(f"no tile for {n} in {candidates}")


def _dot(a, b):
    return jnp.dot(a, b, preferred_element_type=F32)


def _dot_nt(a, b):
    return lax.dot_general(a, b, (((1,), (1,)), ((), ())), preferred_element_type=F32)


def _dot_tn(a, b):
    return lax.dot_general(a, b, (((0,), (0,)), ((), ())), preferred_element_type=F32)


def _split2(x):
    hi = x.astype(BF16)
    lo = (x - hi.astype(F32)).astype(BF16)
    return hi, lo


def _split3(x):
    hi = x.astype(BF16)
    r1 = x - hi.astype(F32)
    mid = r1.astype(BF16)
    lo = (r1 - mid.astype(F32)).astype(BF16)
    return hi, mid, lo


def _rmsnorm_kernel(x_ref, g_ref, o_ref):
    x = x_ref[...]
    ms = jnp.mean(x * x, axis=-1, keepdims=True)
    o_ref[...] = (x * lax.rsqrt(ms + NORM_EPS) * g_ref[...]).astype(o_ref.dtype)


def _rmsnorm(x, g, out_dtype):
    m, d = x.shape
    tm = _pick(m, (512, 256, 128, 64, 32, 16))
    return pl.pallas_call(
        _rmsnorm_kernel,
        grid=(m // tm,),
        in_specs=[pl.BlockSpec((tm, d), lambda i: (i, 0)),
                  pl.BlockSpec((1, d), lambda i: (0, 0))],
        out_specs=pl.BlockSpec((tm, d), lambda i: (i, 0)),
        out_shape=jax.ShapeDtypeStruct((m, d), out_dtype),
        compiler_params=_params("parallel"),
        name="rmsnorm",
    )(x, g.reshape(1, d))


def _rope(acc, cos_ref, sin_ref, rot_half):
    tn = acc.shape[1]
    reps = tn // V7X_LANES
    cos = jnp.concatenate([cos_ref[...]] * reps, axis=1)
    sin = jnp.concatenate([sin_ref[...]] * reps, axis=1)
    lane = lax.broadcasted_iota(jnp.int32, acc.shape, 1) % V7X_LANES
    partner = jnp.where(lane < rot_half,
                        pltpu.roll(acc, tn - rot_half, 1),
                        pltpu.roll(acc, rot_half, 1))
    return acc * cos + partner * sin


def _mm_kernel(*refs, kind, rot_half):
    n_extra = {"plain": 0, "res": 1, "rope": 2}[kind]
    x_ref, w_ref = refs[:2]
    extras = refs[2:2 + n_extra]
    xd_ref = refs[2 + n_extra]
    extras_d = refs[3 + n_extra:3 + 2 * n_extra]
    o_ref, od_ref, wb_ref = refs[3 + 2 * n_extra:]

    def epilogue(acc, ex):
        if kind == "res":
            return ex[0][...] + acc
        if kind == "rope":
            return _rope(acc, ex[0], ex[1], rot_half)
        return acc

    tm = x_ref.shape[0]

    @pl.when(pl.program_id(1) == 0)
    def _():
        wb_ref[...] = w_ref[...].astype(BF16)
        rows = jnp.concatenate([x_ref[...].astype(BF16), xd_ref[...].astype(BF16)], axis=0)
        acc = _dot(rows, wb_ref[...])
        o_ref[...] = epilogue(acc[:tm], extras).astype(o_ref.dtype)
        od_ref[...] = epilogue(acc[tm:], extras_d).astype(od_ref.dtype)

    @pl.when(pl.program_id(1) > 0)
    def _():
        o_ref[...] = epilogue(_dot(x_ref[...].astype(BF16), wb_ref[...]), extras).astype(o_ref.dtype)


def _mm_swiglu_kernel(x_ref, wg_ref, wu_ref, xd_ref, o_ref, od_ref, wgb_ref, wub_ref):
    tm = x_ref.shape[0]

    def gated(x):
        g = _dot(x, wgb_ref[...])
        u = _dot(x, wub_ref[...])
        return g * jax.nn.sigmoid(g) * u

    @pl.when(pl.program_id(1) == 0)
    def _():
        wgb_ref[...] = wg_ref[...].astype(BF16)
        wub_ref[...] = wu_ref[...].astype(BF16)
        h = gated(jnp.concatenate([x_ref[...], xd_ref[...]], axis=0))
        o_ref[...] = h[:tm].astype(o_ref.dtype)
        od_ref[...] = h[tm:].astype(od_ref.dtype)

    @pl.when(pl.program_id(1) > 0)
    def _():
        o_ref[...] = gated(x_ref[...]).astype(o_ref.dtype)


def _mm_tiles(m, k, n):
    tm = _pick(m, (1024, 512, 256, 128, 64, 32, 16))
    tn = _pick(n, (1024, 512, 256, 128))
    if k > 4096:
        tm, tn = min(tm, 512), min(tn, 512)
    return tm, tn


def _matmul(x, xd, w, layer, n, col0=0, out_dtype=F32, res=None, rope=None):
    m, k = x.shape
    md = xd.shape[0]
    tm, tn = _mm_tiles(m, k, n)
    assert col0 % tn == 0 and w.shape[1] == k and xd.shape[1] == k
    cb0 = col0 // tn
    x_spec = pl.BlockSpec((tm, k), lambda j, i: (i, 0))
    xd_spec = pl.BlockSpec((md, k), lambda j, i: (0, 0))
    w_spec = pl.BlockSpec((None, k, tn), lambda j, i: (layer, 0, cb0 + j))
    o_spec = pl.BlockSpec((tm, tn), lambda j, i: (i, j))
    od_spec = pl.BlockSpec((md, tn), lambda j, i: (0, j))
    kind, rot_half, ex, ex_specs, exd, exd_specs = "plain", 0, (), [], (), []
    if res is not None:
        kind, ex, ex_specs, exd, exd_specs = "res", (res[0],), [o_spec], (res[1],), [od_spec]
    elif rope is not None:
        (cos, sin, rot_half), (cos_d, sin_d, _) = rope
        assert cos.shape[0] % tm == 0 and cos_d.shape[0] == md
        period = cos.shape[0] // tm
        kind = "rope"
        ex, ex_specs = (cos, sin), [pl.BlockSpec((tm, V7X_LANES), lambda j, i: (i % period, 0))] * 2
        exd, exd_specs = (cos_d, sin_d), [pl.BlockSpec((md, V7X_LANES), lambda j, i: (0, 0))] * 2
    return pl.pallas_call(
        functools.partial(_mm_kernel, kind=kind, rot_half=rot_half),
        grid=(n // tn, m // tm),
        in_specs=[x_spec, w_spec, *ex_specs, xd_spec, *exd_specs],
        out_specs=[o_spec, od_spec],
        out_shape=[jax.ShapeDtypeStruct((m, n), out_dtype), jax.ShapeDtypeStruct((md, n), out_dtype)],
        scratch_shapes=[pltpu.VMEM((k, tn), BF16)],
        compiler_params=_params("arbitrary", "arbitrary"),
        name="matmul",
    )(x, w, *ex, xd, *exd)


def _swiglu_up(x, xd, w_gate_up, layer):
    m, k = x.shape
    md = xd.shape[0]
    f = w_gate_up.shape[2] // 2
    tm, tn = _mm_tiles(m, k, f)
    nb = f // tn
    return pl.pallas_call(
        _mm_swiglu_kernel,
        grid=(nb, m // tm),
        in_specs=[pl.BlockSpec((tm, k), lambda j, i: (i, 0)),
                  pl.BlockSpec((None, k, tn), lambda j, i: (layer, 0, j)),
                  pl.BlockSpec((None, k, tn), lambda j, i: (layer, 0, nb + j)),
                  pl.BlockSpec((md, k), lambda j, i: (0, 0))],
        out_specs=[pl.BlockSpec((tm, tn), lambda j, i: (i, j)),
                   pl.BlockSpec((md, tn), lambda j, i: (0, j))],
        out_shape=[jax.ShapeDtypeStruct((m, f), BF16), jax.ShapeDtypeStruct((md, f), BF16)],
        scratch_shapes=[pltpu.VMEM((k, tn), BF16), pltpu.VMEM((k, tn), BF16)],
        compiler_params=_params("arbitrary", "arbitrary"),
        name="swiglu_up",
    )(x, w_gate_up, w_gate_up, xd)


def _ffn(x, xd, g, w_gate_up, w_down, layer):
    h, hdec = _rmsnorm(x, g, BF16), _rmsnorm(xd, g, BF16)
    hid, hid_d = _swiglu_up(h, hdec, w_gate_up, layer)
    return _matmul(hid, hid_d, w_down, layer, w_down.shape[2], res=(x, xd))


def _rope_table_kernel(inv_ref, cos_ref, sin_ref, *, base, period, rot_half):
    rows = cos_ref.shape[0]
    row = lax.broadcasted_iota(jnp.int32, (rows, V7X_LANES), 0) + pl.program_id(0) * rows
    lane = lax.broadcasted_iota(jnp.int32, (rows, V7X_LANES), 1)
    pos = (base + (row & (period - 1))).astype(F32)
    ang = pos * inv_ref[...]
    c, s = jnp.cos(ang), jnp.sin(ang)
    cos_ref[...] = jnp.where(lane < 2 * rot_half, c, 1.0)
    sin_ref[...] = jnp.where(lane < rot_half, -s, jnp.where(lane < 2 * rot_half, s, 0.0))


def _rope_tables(rows, base, period, head_dim):
    assert period & (period - 1) == 0 and head_dim == V7X_LANES
    rot_half = head_dim // ROPE_FRACTION // 2
    inv = ROPE_THETA ** (-jnp.arange(rot_half, dtype=F32) / rot_half)
    inv = jnp.concatenate([inv, inv, jnp.zeros((head_dim - 2 * rot_half,), F32)]).reshape(1, head_dim)
    tr = _pick(rows, (256, 128, 64, 32, 16, 8))
    spec = pl.BlockSpec((tr, head_dim), lambda i: (i, 0))
    cos, sin = pl.pallas_call(
        functools.partial(_rope_table_kernel, base=base, period=period, rot_half=rot_half),
        grid=(rows // tr,),
        in_specs=[pl.BlockSpec((1, head_dim), lambda i: (0, 0))],
        out_specs=[spec, spec],
        out_shape=[jax.ShapeDtypeStruct((rows, head_dim), F32)] * 2,
        compiler_params=_params("parallel"),
        name="rope_tables",
    )(inv)
    return cos, sin, rot_half


def _gla_gate_kernel(h_ref, wlr_ref, wg_ref, b_ref, o_ref):
    rank = wg_ref.shape[0]
    glr = _dot(h_ref[...], wlr_ref[:, :rank].astype(BF16))
    z = _dot(glr.astype(BF16), wg_ref[...].astype(BF16)) + b_ref[...]
    o_ref[...] = (jnp.minimum(z, 0.0) - jnp.log1p(jnp.exp(-jnp.abs(z)))) * (1.0 / GLA_GATE_NORMALIZER)


def _gla_gate(h, w_in, layer, col0, w_gate, b_gate):
    m, d = h.shape
    r, kd = w_gate.shape
    assert col0 % V7X_LANES == 0 and col0 + r == w_in.shape[2] and r <= V7X_LANES
    tm = _pick(m, (512, 256, 128, 64, 32, 16))
    return pl.pallas_call(
        _gla_gate_kernel,
        grid=(m // tm,),
        in_specs=[pl.BlockSpec((tm, d), lambda i: (i, 0)),
                  pl.BlockSpec((None, d, V7X_LANES), lambda i: (layer, 0, col0 // V7X_LANES)),
                  pl.BlockSpec((r, kd), lambda i: (0, 0)),
                  pl.BlockSpec((1, kd), lambda i: (0, 0))],
        out_specs=pl.BlockSpec((tm, kd), lambda i: (i, 0)),
        out_shape=jax.ShapeDtypeStruct((m, kd), F32),
        compiler_params=_params("parallel"),
        name="gla_gate",
    )(h, w_in, w_gate, b_gate.reshape(1, kd))


def _block_mean_stream(pt_ref, cache_hbm, km_ref, pbuf, sems, *, layer, step, n_steps, pages_per_block):
    per_step, page = pbuf.shape[1:3]
    table_pages = pt_ref.shape[1]
    cur = lax.rem(step, 2)

    def page_copy(s, half, a):
        flat = s * per_step + a
        phys = pt_ref[flat // table_pages, lax.rem(flat, table_pages)]
        return pltpu.make_async_copy(cache_hbm.at[layer, phys], pbuf.at[half, a], sems.at[half, a])

    def start_fetch(s, half):
        for a in range(per_step):
            page_copy(s, half, a).start()

    @pl.when(step == 0)
    def _():
        start_fetch(step, cur)

    @pl.when(step + 1 < n_steps)
    def _():
        start_fetch(step + 1, 1 - cur)

    for a in range(per_step):
        page_copy(step, cur, a).wait()

    for a0 in range(0, per_step, pages_per_block):
        total = functools.reduce(jnp.add, [jnp.sum(pbuf[cur, a], axis=0)
                                           for a in range(a0, a0 + pages_per_block)])
        flat = step * per_step + a0
        block = lax.rem(flat, table_pages) // pages_per_block
        km_ref[flat // table_pages, block] = total * (1.0 / (pages_per_block * page))


def _gla_kernel(*refs, chunk, sub, n_valid, has_state, side):
    refs = list(refs)
    pt_ref = refs.pop(0) if side else None
    q_ref, k_ref, v_ref, g_ref, r_ref, hn_ref = refs[:6]
    del refs[:6]
    s0_ref = refs.pop(0) if has_state else None
    cache_hbm = refs.pop(0) if side else None
    o_ref, s_ref = refs[:2]
    del refs[:2]
    km_ref = refs.pop(0) if side else None
    st_ref = refs.pop(0)
    ci = pl.program_id(2)
    span, dk = q_ref.shape
    half = sub // 2

    if side:
        pbuf, sems = refs
        step = (pl.program_id(0) * pl.num_programs(1) + pl.program_id(1)) * pl.num_programs(2) + ci
        n_steps = pl.num_programs(0) * pl.num_programs(1) * pl.num_programs(2)
        _block_mean_stream(pt_ref, cache_hbm, km_ref, pbuf, sems, layer=side[0], step=step,
                           n_steps=n_steps, pages_per_block=side[1])

    @pl.when(ci == 0)
    def _():
        if s0_ref is None:
            st_ref[...] = jnp.zeros(st_ref.shape, st_ref.dtype)
        else:
            st_ref[...] = s0_ref[0, 0].T

    row = lax.broadcasted_iota(jnp.int32, (chunk, chunk), 0)
    col = lax.broadcasted_iota(jnp.int32, (chunk, chunk), 1)
    tri = jnp.where(col <= row, 1.0, 0.0).astype(BF16)
    krow = lax.broadcasted_iota(jnp.int32, (chunk, 1), 0)
    hrow = lax.broadcasted_iota(jnp.int32, (half, 1), 0)
    hcol = lax.broadcasted_iota(jnp.int32, (half, chunk), 1)
    hn = hn_ref[...]

    def within_chunk(c0):
        q = q_ref[c0:c0 + chunk, :] * (dk ** -0.5)
        k = k_ref[c0:c0 + chunk, :]
        v = v_ref[c0:c0 + chunk, :].astype(BF16)
        tok = krow + (ci * span + c0)
        g = jnp.where(tok < n_valid, g_ref[c0:c0 + chunk, :], 0.0) * LOG2_E
        g_hi, g_mid, g_lo = _split3(g)
        b = _dot(tri, g_hi) + _dot(tri, g_mid) + _dot(tri, g_lo)
        b_last = b[chunk - 1:chunk, :]

        blocks = []
        for i in range(chunk // sub):
            lo = i * sub
            bi, qi, ki = b[lo:lo + sub], q[lo:lo + sub], k[lo:lo + sub]
            for hh in range(2):
                bh, qh = bi[hh * half:(hh + 1) * half], qi[hh * half:(hh + 1) * half]
                a = jnp.zeros((half, chunk), F32)
                for jj in range((hh + 1) * half):
                    diff = bh - bi[jj:jj + 1]
                    if jj >= hh * half:
                        diff = jnp.where(hrow >= jj - hh * half, diff, NEG_INF)
                    pair = jnp.sum(qh * jnp.exp2(diff) * ki[jj:jj + 1], axis=-1, keepdims=True)
                    a = jnp.where(hcol == lo + jj, pair, a)
                blocks.append(a)
            if i > 0:
                ref_b = bi[0:1]
                qs = (qi * jnp.exp2(bi - ref_b)).astype(BF16)
                ks = (k * jnp.exp2(jnp.where(krow < lo, ref_b - b, NEG_INF))).astype(BF16)
                far = _dot_nt(qs, ks)
                blocks[-2] = blocks[-2] + far[:half]
                blocks[-1] = blocks[-1] + far[half:]
        attn = jnp.concatenate(blocks, axis=0)
        o_intra = _dot(attn.astype(BF16), v)
        q_dec = (q * jnp.exp2(b)).astype(BF16)
        k_dec = (k * jnp.exp2(b_last - b)).astype(BF16)
        r = r_ref[c0:c0 + chunk, :]
        return o_intra, q_dec, _dot_tn(v, k_dec), jnp.exp2(b_last), r * jax.nn.sigmoid(r)

    def through_state(c0, o_intra, q_dec, kv, decay, out_gate):
        st = st_ref[...]
        o = o_intra + _dot_nt(q_dec, st.astype(BF16))
        st_ref[...] = st * decay + kv
        ms = jnp.mean(o * o, axis=-1, keepdims=True)
        y = o * lax.rsqrt(ms + NORM_EPS) * hn
        o_ref[c0:c0 + chunk, :] = (y * out_gate).astype(o_ref.dtype)

    starts = list(range(0, span, chunk))
    ready = within_chunk(starts[0])
    for n, c0 in enumerate(starts):
        upcoming = within_chunk(starts[n + 1]) if n + 1 < len(starts) else None
        through_state(c0, *ready)
        ready = upcoming

    @pl.when(ci == pl.num_programs(2) - 1)
    def _():
        s_ref[0, 0] = st_ref[...].T


def _gla_core(proj, g, head_norm, state, layer, dims, seq, n_valid, block_means_of=None):
    m = proj.shape[0]
    bsz, heads, dk, dv = dims
    chunk = GLA_CHUNK
    assert seq % chunk == 0 and chunk % GLA_SUB == 0 and GLA_SUB % 16 == 0 and (2 * heads * dk) % dv == 0
    span = chunk * _pick(seq // chunk, (GLA_CHUNKS_PER_STEP, 2, 1))
    nc = seq // span
    kb0 = heads
    vb0 = 2 * heads * dk // dv
    rb0 = vb0 + heads

    def rows(bi, hi, ci):
        return bi * nc + ci

    in_specs = [pl.BlockSpec((span, dk), lambda bi, hi, ci, *_: (rows(bi, hi, ci), hi)),
                pl.BlockSpec((span, dk), lambda bi, hi, ci, *_: (rows(bi, hi, ci), kb0 + hi)),
                pl.BlockSpec((span, dv), lambda bi, hi, ci, *_: (rows(bi, hi, ci), vb0 + hi)),
                pl.BlockSpec((span, dk), lambda bi, hi, ci, *_: (rows(bi, hi, ci), hi)),
                pl.BlockSpec((span, dv), lambda bi, hi, ci, *_: (rows(bi, hi, ci), rb0 + hi)),
                pl.BlockSpec((1, dv), lambda bi, hi, ci, *_: (0, 0))]
    operands = [proj, proj, proj, g, proj, head_norm.reshape(1, dv)]
    out_specs = [pl.BlockSpec((span, dv), lambda bi, hi, ci, *_: (rows(bi, hi, ci), hi)),
                 pl.BlockSpec((1, 1, dk, dv), lambda bi, hi, ci, *_: (bi, hi, 0, 0))]
    out_shape = [jax.ShapeDtypeStruct((m, heads * dv), BF16),
                 jax.ShapeDtypeStruct((bsz, heads, dk, dv), F32)]
    scratch = [pltpu.VMEM((dv, dk), F32)]
    prefetch = []
    if state is not None:
        in_specs.append(pl.BlockSpec((None, 1, 1, dk, dv), lambda bi, hi, ci, *_: (layer, bi, hi, 0, 0)))
        operands.append(state)
    side = None
    if block_means_of is not None:
        cache, page_table = block_means_of
        _, _, page, c_heads, c_hd = cache.shape
        c_bsz, table_pages = page_table.shape
        ppb = MOBA_BLOCK // page
        n_steps = bsz * heads * nc
        per_step = c_bsz * table_pages // n_steps
        assert per_step * n_steps == c_bsz * table_pages and per_step % ppb == 0 and table_pages % per_step == 0
        side = (layer, ppb)
        prefetch = [page_table]
        in_specs.append(pl.BlockSpec(memory_space=pl.ANY))
        operands.append(cache)
        km_shape = (c_bsz, table_pages // ppb, c_heads, c_hd)
        out_specs.append(pl.BlockSpec(km_shape, lambda bi, hi, ci, *_: (0, 0, 0, 0)))
        out_shape.append(jax.ShapeDtypeStruct(km_shape, F32))
        scratch += [pltpu.VMEM((2, per_step, page, c_heads, c_hd), F32), pltpu.SemaphoreType.DMA((2, per_step))]
    grid_spec = pltpu.PrefetchScalarGridSpec(
        num_scalar_prefetch=len(prefetch), grid=(bsz, heads, nc),
        in_specs=in_specs, out_specs=out_specs, scratch_shapes=scratch)
    return pl.pallas_call(
        functools.partial(_gla_kernel, chunk=chunk, sub=GLA_SUB, n_valid=n_valid,
                          has_state=state is not None, side=side),
        grid_spec=grid_spec,
        out_shape=out_shape,
        compiler_params=_params("arbitrary", "arbitrary", "arbitrary"),
        name="gla_core",
    )(*prefetch, *operands)


def _moba_prompt_kernel(q_ref, k_ref, v_ref, o_ref, kb_ref, vt_ref, km_ref, *, blk, top_k):
    seq, hd = q_ref.shape
    nb = seq // blk
    score_scale = (hd ** -0.5) * LOG2_E

    for n in range(nb):
        kblk = k_ref[n * blk:(n + 1) * blk, :]
        km_ref[n:n + 1, :] = jnp.mean(kblk, axis=0, keepdims=True)
        kb_ref[n * blk:(n + 1) * blk, :] = kblk.astype(BF16)
        vt_ref[:, n * blk:(n + 1) * blk] = v_ref[n * blk:(n + 1) * blk, :].T.astype(BF16)
    km_hi, km_lo = _split2(km_ref[...])

    brow = lax.broadcasted_iota(jnp.int32, (nb, blk), 0)
    key = lax.broadcasted_iota(jnp.int32, (blk, blk), 0)
    qry = lax.broadcasted_iota(jnp.int32, (blk, blk), 1)
    causal = jnp.where(key <= qry, 0.0, NEG_INF)

    def scores(i):
        q = q_ref[i * blk:(i + 1) * blk, :]
        s = _dot_nt(kb_ref[0:(i + 1) * blk, :], q.astype(BF16)) * score_scale
        pieces = []
        if i > 0:
            q_hi, q_lo = _split2(q)
            gate = _dot_nt(km_hi, q_hi) + _dot_nt(km_lo, q_hi) + _dot_nt(km_hi, q_lo)
            past = brow < i
            gate = jnp.where(past, gate, NEG_INF)
            rank = jnp.zeros((nb, blk), jnp.int32)
            for n in range(i):
                gn = gate[n:n + 1, :]
                rank = rank + jnp.where(gn > gate, 1, jnp.where(gn == gate, jnp.where(brow > n, 1, 0), 0))
            bias = jnp.where(past, jnp.where(rank < top_k, 0.0, NEG_INF), NEG_INF)
            pieces = [s[n * blk:(n + 1) * blk] + bias[n:n + 1] for n in range(i)]
        return pieces + [s[i * blk:(i + 1) * blk] + causal]

    def softmax(pieces):
        m = functools.reduce(jnp.maximum, [jnp.max(p, axis=0, keepdims=True) for p in pieces])
        probs = [jnp.exp2(p - m) for p in pieces]
        l = functools.reduce(jnp.add, [jnp.sum(p, axis=0, keepdims=True) for p in probs])
        return jnp.concatenate([p.astype(BF16) for p in probs], axis=0), l

    def attend(i, probs, l):
        acc = _dot(vt_ref[:, 0:(i + 1) * blk], probs)
        o_ref[i * blk:(i + 1) * blk, :] = (acc / l).T.astype(o_ref.dtype)

    pending = scores(0)
    for i in range(nb):
        upcoming = scores(i + 1) if i + 1 < nb else None
        probs, l = softmax(pending)
        attend(i, probs, l)
        pending = upcoming


def _moba_prompt(q, k, v, bsz, seq, heads):
    m, d = q.shape
    hd = d // heads
    blk = MOBA_BLOCK
    assert seq % blk == 0 and hd == V7X_LANES
    nb = seq // blk
    spec = pl.BlockSpec((seq, hd), lambda bi, hi: (bi, hi))
    return pl.pallas_call(
        functools.partial(_moba_prompt_kernel, blk=blk, top_k=MOBA_TOP_K),
        grid=(bsz, heads),
        in_specs=[spec, spec, spec],
        out_specs=spec,
        out_shape=jax.ShapeDtypeStruct((m, d), BF16),
        scratch_shapes=[pltpu.VMEM((seq, hd), BF16),
                        pltpu.VMEM((hd, seq), BF16),
                        pltpu.VMEM((nb, hd), F32)],
        compiler_params=_params("parallel", "parallel"),
        name="moba_prompt",
    )(q, k, v)


DECODE_Q_ROWS = 16


def _moba_select_kernel(q_ref, km_ref, o_ref, *, top_k, n_q):
    km = km_ref[0]
    nbp, heads, _ = km.shape
    bidx = lax.broadcasted_iota(jnp.int32, (nbp, heads, 1), 0)
    for t in range(n_q):
        gate = jnp.sum(km * q_ref[0, t][None], axis=-1, keepdims=True)
        for r in range(top_k):
            best = jnp.max(gate, axis=0, keepdims=True)
            idx = jnp.min(jnp.where(gate == best, bidx, nbp), axis=0, keepdims=True)
            o_ref[0, t * top_k + r] = jnp.broadcast_to(idx[0], o_ref.shape[2:])
            gate = jnp.where(bidx == idx, NEG_INF, gate)


def _moba_select(q, kmean, n_q):
    bsz, _, heads, hd = q.shape
    nbp = kmean.shape[1]
    slots = n_q * MOBA_TOP_K
    sel = pl.pallas_call(
        functools.partial(_moba_select_kernel, top_k=MOBA_TOP_K, n_q=n_q),
        grid=(bsz,),
        in_specs=[pl.BlockSpec((1, n_q, heads, hd), lambda bi: (bi, 0, 0, 0)),
                  pl.BlockSpec((1, nbp, heads, hd), lambda bi: (bi, 0, 0, 0))],
        out_specs=pl.BlockSpec((1, slots, heads, V7X_LANES), lambda bi: (bi, 0, 0, 0)),
        out_shape=jax.ShapeDtypeStruct((bsz, slots, heads, V7X_LANES), jnp.int32),
        compiler_params=_params("parallel"),
        name="moba_select",
    )(q, kmean)
    return sel[..., 0]


def _moba_decode_kernel(pt_ref, sel_ref, q_ref, kn_ref, vn_ref, kc_hbm, vc_hbm, o_ref,
                        kbuf, vbuf, sems, *, layer, heads, top_k, pages_per_block, n_q):
    b = pl.program_id(0)
    h = pl.program_id(1)
    step = b * heads + h
    n_steps = pl.num_programs(0) * heads
    cur = lax.rem(step, 2)
    n_slots = n_q * top_k
    n_pages = n_slots * pages_per_block
    rows, hd = q_ref.shape[1:]
    page = kn_ref.shape[1]
    scale = hd ** -0.5
    per_query = top_k * pages_per_block

    def page_copies(bb, hh, half, i):
        slot, pg = divmod(i, pages_per_block)
        t, j = divmod(i, per_query)
        blk = sel_ref[(bb * n_slots + slot) * heads + hh]
        phys = pt_ref[bb, blk * pages_per_block + pg]
        dst = pl.ds(j * page, page)
        return (pltpu.make_async_copy(kc_hbm.at[layer, phys, :, hh, :], kbuf.at[half, t, dst], sems.at[0, half, i]),
                pltpu.make_async_copy(vc_hbm.at[layer, phys, :, hh, :], vbuf.at[half, t, dst], sems.at[1, half, i]))

    def start_gather(bb, hh, half):
        for i in range(n_pages):
            for c in page_copies(bb, hh, half, i):
                c.start()

    @pl.when(step == 0)
    def _():
        start_gather(b, h, cur)

    @pl.when(step + 1 < n_steps)
    def _():
        nxt = step + 1
        start_gather(nxt // heads, lax.rem(nxt, heads), 1 - cur)

    for i in range(n_pages):
        for c in page_copies(b, h, cur, i):
            c.wait()

    qb = q_ref[0].astype(BF16)
    lane = lax.broadcasted_iota(jnp.int32, (1, page), 1)
    rowi = lax.broadcasted_iota(jnp.int32, (rows, 1), 0)
    s_own = _dot_nt(qb, kn_ref[0].astype(BF16)) * scale
    v_own = vn_ref[0].astype(BF16)
    s_past = [_dot_nt(qb, kbuf[cur, t].astype(BF16)) * scale for t in range(n_q)]
    s_new = [jnp.where(lane <= t, s_own, NEG_INF) for t in range(n_q)]
    m = [jnp.maximum(jnp.max(sp, axis=1, keepdims=True), jnp.max(sn, axis=1, keepdims=True))
         for sp, sn in zip(s_past, s_new)]
    p_past = [jnp.exp(sp - mt) for sp, mt in zip(s_past, m)]
    p_new = [jnp.exp(sn - mt) for sn, mt in zip(s_new, m)]
    l = [jnp.sum(pp, axis=1, keepdims=True) + jnp.sum(pn, axis=1, keepdims=True)
         for pp, pn in zip(p_past, p_new)]
    acc = [_dot(pp.astype(BF16), vbuf[cur, t].astype(BF16)) + _dot(pn.astype(BF16), v_own)
           for t, (pp, pn) in enumerate(zip(p_past, p_new))]
    out = jnp.zeros((rows, hd), F32)
    for t in range(n_q):
        out = jnp.where(rowi == t, acc[t] / l[t], out)
    o_ref[0] = out


def _moba_decode(q, k_new, v_new, cache_k, cache_v, page_table, sel, layer, n_q):
    bsz, rows, d = q.shape
    _, _, page, heads, hd = cache_k.shape
    ppb = MOBA_BLOCK // page
    top_k = MOBA_TOP_K
    n_pages = n_q * top_k * ppb
    assert n_q <= rows and k_new.shape[1] == page
    q_spec = pl.BlockSpec((1, rows, hd), lambda bi, hi, pt, sl: (bi, 0, hi))
    new_spec = pl.BlockSpec((1, page, hd), lambda bi, hi, pt, sl: (bi, 0, hi))
    any_spec = pl.BlockSpec(memory_space=pl.ANY)
    grid_spec = pltpu.PrefetchScalarGridSpec(
        num_scalar_prefetch=2,
        grid=(bsz, heads),
        in_specs=[q_spec, new_spec, new_spec, any_spec, any_spec],
        out_specs=q_spec,
        scratch_shapes=[pltpu.VMEM((2, n_q, top_k * ppb * page, hd), F32),
                        pltpu.VMEM((2, n_q, top_k * ppb * page, hd), F32),
                        pltpu.SemaphoreType.DMA((2, 2, n_pages))],
    )
    return pl.pallas_call(
        functools.partial(_moba_decode_kernel, layer=layer, heads=heads, top_k=top_k,
                          pages_per_block=ppb, n_q=n_q),
        grid_spec=grid_spec,
        out_shape=jax.ShapeDtypeStruct((bsz, rows, d), F32),
        compiler_params=_params("arbitrary", "arbitrary"),
        name="moba_decode",
    )(page_table, sel.reshape(-1), q, k_new, v_new, cache_k, cache_v)


def kernel(x_prompt, x_sample, state_gla, cache_k, cache_v, page_table, norm_mix, norm_ffn, norm_final,
           gla_w_in, gla_w_gate, gla_b_gate, gla_head_norm, gla_w_out, moba_w_qkv, moba_w_o,
           ffn_w_gate_up, ffn_w_down):
    bp, seq, d = x_prompt.shape
    bd, sd, _ = x_sample.shape
    depth = norm_mix.shape[0]
    n_moba, n_pool, page, mh, mhd = cache_k.shape
    past_len = page_table.shape[1] * page
    assert past_len % MOBA_BLOCK == 0 and sd <= min(MOBA_BLOCK, 8) and GLA_CHUNK % sd == 0
    gh, gdk, gdv = state_gla.shape[2:]

    xp = x_prompt.reshape(bp * seq, d)
    xs = x_sample.reshape(bd * sd, d)
    rope_p = _rope_tables(seq, 0, seq, mhd)
    rope_s = _rope_tables(bd * sd, past_len, sd, mhd)
    kmean = {}
    pad_s = GLA_CHUNK - sd

    def pad_tokens(a, n):
        c = a.shape[1]
        return jnp.pad(a.reshape(bd, sd, c), ((0, 0), (0, n), (0, 0))).reshape(bd * (sd + n), c)

    gla_sp, gla_ss, kp_l, vp_l, ks_l, vs_l = [], [], [], [], [], []
    for i in range(depth):
        j = i // 2
        hp = _rmsnorm(xp, norm_mix[i], BF16)
        hs = _rmsnorm(xs, norm_mix[i], BF16)
        if i % 2 == 0:
            n_main = 2 * gh * gdk + 2 * gh * gdv
            proj_p, proj_s = _matmul(hp, hs, gla_w_in, j, n_main)
            g_p = _gla_gate(hp, gla_w_in, j, n_main, gla_w_gate[j], gla_b_gate[j])
            g_s = _gla_gate(hs, gla_w_in, j, n_main, gla_w_gate[j], gla_b_gate[j])
            assert i + 1 < depth and j < n_moba
            op, sp, kmean[j] = _gla_core(proj_p, g_p, gla_head_norm[j], None, j, (bp, gh, gdk, gdv), seq, seq,
                                         block_means_of=(cache_k, page_table))
            os_pad, ss = _gla_core(pad_tokens(proj_s, pad_s), pad_tokens(g_s, pad_s), gla_head_norm[j],
                                   state_gla, j, (bd, gh, gdk, gdv), sd + pad_s, sd)
            os_ = os_pad.reshape(bd, sd + pad_s, d)[:, :sd].reshape(bd * sd, d)
            xp, xs = _matmul(op, os_, gla_w_out, j, d, res=(xp, xs))
            gla_sp.append(sp)
            gla_ss.append(ss)
        else:
            rope = (rope_p, rope_s)
            qp, qs = _matmul(hp, hs, moba_w_qkv, j, d, 0, rope=rope)
            kp, ks = _matmul(hp, hs, moba_w_qkv, j, d, d, rope=rope)
            vp, vs = _matmul(hp, hs, moba_w_qkv, j, d, 2 * d)
            op = _moba_prompt(qp, kp, vp, bp, seq, mh)
            q3 = pad_tokens(qs, DECODE_Q_ROWS - sd).reshape(bd, DECODE_Q_ROWS, d)
            k3, v3 = (pad_tokens(a, page - sd).reshape(bd, page, d) for a in (ks, vs))
            sel = _moba_select(qs.reshape(bd, sd, mh, mhd), kmean[j], sd)
            o3 = _moba_decode(q3, k3, v3, cache_k, cache_v, page_table, sel, j, sd)
            xp, xs = _matmul(op, o3[:, :sd].reshape(bd * sd, d), moba_w_o, j, d, res=(xp, xs))
            kp_l.append(kp)
            vp_l.append(vp)
            ks_l.append(ks)
            vs_l.append(vs)
        xp, xs = _ffn(xp, xs, norm_ffn[i], ffn_w_gate_up, ffn_w_down, i)

    y_prompt = _rmsnorm(xp, norm_final, F32).reshape(bp, seq, d)
    y_sample = _rmsnorm(xs, norm_final, F32).reshape(bd, sd, d)
    def pages(rows_per_layer, lead):
        return jnp.stack(rows_per_layer).reshape(len(rows_per_layer), *lead, mh, mhd)

    return (y_prompt, y_sample, jnp.stack(gla_sp), jnp.stack(gla_ss),
            pages(kp_l, (bp * seq // page, page)), pages(vp_l, (bp * seq // page, page)),
            pages(ks_l, (bd, sd)), pages(vs_l, (bd, sd)))
```

```python
import functools

import jax
import jax.numpy as jnp
from jax import lax
from jax.experimental import pallas as pl
from jax.experimental.pallas import tpu as pltpu

NORM_EPS = 1e-6
GLA_HEADS = 4
GLA_GATE_RANK = 16
GLA_GATE_NORMALIZER = 16.0
GLA_CHUNK = 64
GLA_SUB = 16
GLA_CHUNKS_PER_STEP = 4
MOBA_HEADS = 16
MOBA_BLOCK = 256
MOBA_TOP_K = 3
ROPE_THETA = 500000.0
ROPE_FRACTION = 4
PAGE_SIZE = 128

V7X_LANES = 128
V7X_VMEM_LIMIT_BYTES = 56 * 1024 * 1024

F32 = jnp.float32
BF16 = jnp.bfloat16
NEG_INF = float("-inf")
LOG2_E = 1.4426950408889634


def _params(*sem):
    return pltpu.CompilerParams(dimension_semantics=sem, vmem_limit_bytes=V7X_VMEM_LIMIT_BYTES)


def _pick(n, candidates):
    for c in candidates:
        if n % c == 0:
            return c
    raise ValueError(f"no tile for {n} in {candidates}")


def _dot(a, b):
    return jnp.dot(a, b, preferred_element_type=F32)


def _dot_nt(a, b):
    return lax.dot_general(a, b, (((1,), (1,)), ((), ())), preferred_element_type=F32)


def _dot_tn(a, b):
    return lax.dot_general(a, b, (((0,), (0,)), ((), ())), preferred_element_type=F32)


def _split2(x):
    hi = x.astype(BF16)
    lo = (x - hi.astype(F32)).astype(BF16)
    return hi, lo


def _split3(x):
    hi = x.astype(BF16)
    r1 = x - hi.astype(F32)
    mid = r1.astype(BF16)
    lo = (r1 - mid.astype(F32)).astype(BF16)
    return hi, mid, lo


def _rmsnorm_kernel(x_ref, g_ref, o_ref):
    x = x_ref[...]
    ms = jnp.mean(x * x, axis=-1, keepdims=True)
    o_ref[...] = (x * lax.rsqrt(ms + NORM_EPS) * g_ref[...]).astype(o_ref.dtype)


def _rmsnorm(x, g, out_dtype):
    m, d = x.shape
    tm = _pick(m, (512, 256, 128, 64, 32, 16))
    return pl.pallas_call(
        _rmsnorm_kernel,
        grid=(m // tm,),
        in_specs=[pl.BlockSpec((tm, d), lambda i: (i, 0)),
                  pl.BlockSpec((1, d), lambda i: (0, 0))],
        out_specs=pl.BlockSpec((tm, d), lambda i: (i, 0)),
        out_shape=jax.ShapeDtypeStruct((m, d), out_dtype),
        compiler_params=_params("parallel"),
        name="rmsnorm",
    )(x, g.reshape(1, d))


def _rope(acc, cos_ref, sin_ref, rot_half):
    tn = acc.shape[1]
    reps = tn // V7X_LANES
    cos = jnp.concatenate([cos_ref[...]] * reps, axis=1)
    sin = jnp.concatenate([sin_ref[...]] * reps, axis=1)
    lane = lax.broadcasted_iota(jnp.int32, acc.shape, 1) % V7X_LANES
    partner = jnp.where(lane < rot_half,
                        pltpu.roll(acc, tn - rot_half, 1),
                        pltpu.roll(acc, rot_half, 1))
    return acc * cos + partner * sin


def _mm_kernel(*refs, kind, rot_half, with_ssq):
    n_extra = {"plain": 0, "res": 1, "rope": 2}[kind]
    x_ref, w_ref = refs[:2]
    extras = refs[2:2 + n_extra]
    xd_ref = refs[2 + n_extra]
    extras_d = refs[3 + n_extra:3 + 2 * n_extra]
    outs = refs[3 + 2 * n_extra:]
    if with_ssq:
        o_ref, od_ref, ssq_ref, ssqd_ref, wb_ref = outs
    else:
        (o_ref, od_ref, wb_ref), ssq_ref, ssqd_ref = outs, None, None

    def finish(acc, ex, out_ref, sq_ref):
        if kind == "res":
            acc = ex[0][...] + acc
        elif kind == "rope":
            acc = _rope(acc, ex[0], ex[1], rot_half)
        out_ref[...] = acc.astype(out_ref.dtype)
        if sq_ref is not None:
            sq_ref[...] = jnp.broadcast_to(jnp.sum(acc * acc, axis=-1, keepdims=True), sq_ref.shape)

    tm = x_ref.shape[0]

    @pl.when(pl.program_id(1) == 0)
    def _():
        wb_ref[...] = w_ref[...].astype(BF16)
        rows = jnp.concatenate([x_ref[...].astype(BF16), xd_ref[...].astype(BF16)], axis=0)
        acc = _dot(rows, wb_ref[...])
        finish(acc[:tm], extras, o_ref, ssq_ref)
        finish(acc[tm:], extras_d, od_ref, ssqd_ref)

    @pl.when(pl.program_id(1) > 0)
    def _():
        finish(_dot(x_ref[...].astype(BF16), wb_ref[...]), extras, o_ref, ssq_ref)


def _mm_swiglu_kernel(x_ref, ssq_ref, gain_ref, wg_ref, wu_ref, xd_ref, ssqd_ref, o_ref, od_ref, wgb_ref, wub_ref):
    tm, k = x_ref.shape

    def normed(x, ssq):
        total = functools.reduce(jnp.add, [ssq[:, c:c + 1] for c in range(0, ssq.shape[1], V7X_LANES)])
        return (x * lax.rsqrt(total * (1.0 / k) + NORM_EPS) * gain_ref[...]).astype(BF16)

    def gated(h):
        g = _dot(h, wgb_ref[...])
        u = _dot(h, wub_ref[...])
        return g * jax.nn.sigmoid(g) * u

    @pl.when(pl.program_id(1) == 0)
    def _():
        wgb_ref[...] = wg_ref[...].astype(BF16)
        wub_ref[...] = wu_ref[...].astype(BF16)
        h = jnp.concatenate([normed(x_ref[...], ssq_ref[...]), normed(xd_ref[...], ssqd_ref[...])], axis=0)
        y = gated(h)
        o_ref[...] = y[:tm].astype(o_ref.dtype)
        od_ref[...] = y[tm:].astype(od_ref.dtype)

    @pl.when(pl.program_id(1) > 0)
    def _():
        o_ref[...] = gated(normed(x_ref[...], ssq_ref[...])).astype(o_ref.dtype)


def _mm_tiles(m, k, n):
    tm = _pick(m, (1024, 512, 256, 128, 64, 32, 16))
    tn = _pick(n, (1024, 512, 256, 128))
    if k > 4096:
        tm, tn = min(tm, 512), min(tn, 512)
    return tm, tn


def _matmul(x, xd, w, layer, n, col0=0, out_dtype=F32, res=None, rope=None, with_ssq=False):
    m, k = x.shape
    md = xd.shape[0]
    tm, tn = _mm_tiles(m, k, n)
    assert col0 % tn == 0 and w.shape[1] == k and xd.shape[1] == k
    cb0 = col0 // tn
    x_spec = pl.BlockSpec((tm, k), lambda j, i: (i, 0))
    xd_spec = pl.BlockSpec((md, k), lambda j, i: (0, 0))
    w_spec = pl.BlockSpec((None, k, tn), lambda j, i: (layer, 0, cb0 + j))
    o_spec = pl.BlockSpec((tm, tn), lambda j, i: (i, j))
    od_spec = pl.BlockSpec((md, tn), lambda j, i: (0, j))
    kind, rot_half, ex, ex_specs, exd, exd_specs = "plain", 0, (), [], (), []
    if res is not None:
        kind, ex, ex_specs, exd, exd_specs = "res", (res[0],), [o_spec], (res[1],), [od_spec]
    elif rope is not None:
        (cos, sin, rot_half), (cos_d, sin_d, _) = rope
        assert cos.shape[0] % tm == 0 and cos_d.shape[0] == md
        period = cos.shape[0] // tm
        kind = "rope"
        ex, ex_specs = (cos, sin), [pl.BlockSpec((tm, V7X_LANES), lambda j, i: (i % period, 0))] * 2
        exd, exd_specs = (cos_d, sin_d), [pl.BlockSpec((md, V7X_LANES), lambda j, i: (0, 0))] * 2
    out_specs = [o_spec, od_spec]
    out_shape = [jax.ShapeDtypeStruct((m, n), out_dtype), jax.ShapeDtypeStruct((md, n), out_dtype)]
    if with_ssq:
        out_specs += [pl.BlockSpec((tm, V7X_LANES), lambda j, i: (i, j)),
                      pl.BlockSpec((md, V7X_LANES), lambda j, i: (0, j))]
        out_shape += [jax.ShapeDtypeStruct((m, n // tn * V7X_LANES), F32),
                      jax.ShapeDtypeStruct((md, n // tn * V7X_LANES), F32)]
    return pl.pallas_call(
        functools.partial(_mm_kernel, kind=kind, rot_half=rot_half, with_ssq=with_ssq),
        grid=(n // tn, m // tm),
        in_specs=[x_spec, w_spec, *ex_specs, xd_spec, *exd_specs],
        out_specs=out_specs,
        out_shape=out_shape,
        scratch_shapes=[pltpu.VMEM((k, tn), BF16)],
        compiler_params=_params("arbitrary", "arbitrary"),
        name="matmul",
    )(x, w, *ex, xd, *exd)


def _swiglu_up(x, xd, ssq, ssq_d, gain, w_gate_up, layer):
    m, k = x.shape
    md = xd.shape[0]
    f = w_gate_up.shape[2] // 2
    tm, tn = _mm_tiles(m, k, f)
    nb = f // tn
    groups = ssq.shape[1]
    return pl.pallas_call(
        _mm_swiglu_kernel,
        grid=(nb, m // tm),
        in_specs=[pl.BlockSpec((tm, k), lambda j, i: (i, 0)),
                  pl.BlockSpec((tm, groups), lambda j, i: (i, 0)),
                  pl.BlockSpec((1, k), lambda j, i: (0, 0)),
                  pl.BlockSpec((None, k, tn), lambda j, i: (layer, 0, j)),
                  pl.BlockSpec((None, k, tn), lambda j, i: (layer, 0, nb + j)),
                  pl.BlockSpec((md, k), lambda j, i: (0, 0)),
                  pl.BlockSpec((md, groups), lambda j, i: (0, 0))],
        out_specs=[pl.BlockSpec((tm, tn), lambda j, i: (i, j)),
                   pl.BlockSpec((md, tn), lambda j, i: (0, j))],
        out_shape=[jax.ShapeDtypeStruct((m, f), BF16), jax.ShapeDtypeStruct((md, f), BF16)],
        scratch_shapes=[pltpu.VMEM((k, tn), BF16), pltpu.VMEM((k, tn), BF16)],
        compiler_params=_params("arbitrary", "arbitrary"),
        name="swiglu_up",
    )(x, ssq, gain.reshape(1, k), w_gate_up, w_gate_up, xd, ssq_d)


def _ffn(x, xd, ssq, ssq_d, g, w_gate_up, w_down, layer):
    hid, hid_d = _swiglu_up(x, xd, ssq, ssq_d, g, w_gate_up, layer)
    return _matmul(hid, hid_d, w_down, layer, w_down.shape[2], res=(x, xd))


def _rope_table_kernel(inv_ref, cos_ref, sin_ref, *, base, period, rot_half):
    rows = cos_ref.shape[0]
    row = lax.broadcasted_iota(jnp.int32, (rows, V7X_LANES), 0) + pl.program_id(0) * rows
    lane = lax.broadcasted_iota(jnp.int32, (rows, V7X_LANES), 1)
    pos = (base + (row & (period - 1))).astype(F32)
    ang = pos * inv_ref[...]
    c, s = jnp.cos(ang), jnp.sin(ang)
    cos_ref[...] = jnp.where(lane < 2 * rot_half, c, 1.0)
    sin_ref[...] = jnp.where(lane < rot_half, -s, jnp.where(lane < 2 * rot_half, s, 0.0))


def _rope_tables(rows, base, period, head_dim):
    assert period & (period - 1) == 0 and head_dim == V7X_LANES
    rot_half = head_dim // ROPE_FRACTION // 2
    inv = ROPE_THETA ** (-jnp.arange(rot_half, dtype=F32) / rot_half)
    inv = jnp.concatenate([inv, inv, jnp.zeros((head_dim - 2 * rot_half,), F32)]).reshape(1, head_dim)
    tr = _pick(rows, (256, 128, 64, 32, 16, 8))
    spec = pl.BlockSpec((tr, head_dim), lambda i: (i, 0))
    cos, sin = pl.pallas_call(
        functools.partial(_rope_table_kernel, base=base, period=period, rot_half=rot_half),
        grid=(rows // tr,),
        in_specs=[pl.BlockSpec((1, head_dim), lambda i: (0, 0))],
        out_specs=[spec, spec],
        out_shape=[jax.ShapeDtypeStruct((rows, head_dim), F32)] * 2,
        compiler_params=_params("parallel"),
        name="rope_tables",
    )(inv)
    return cos, sin, rot_half


def _gla_gate_kernel(h_ref, wlr_ref, wg_ref, b_ref, o_ref):
    rank = wg_ref.shape[0]
    glr = _dot(h_ref[...], wlr_ref[:, :rank].astype(BF16))
    z = _dot(glr.astype(BF16), wg_ref[...].astype(BF16)) + b_ref[...]
    o_ref[...] = (jnp.minimum(z, 0.0) - jnp.log1p(jnp.exp(-jnp.abs(z)))) * (1.0 / GLA_GATE_NORMALIZER)


def _gla_gate(h, w_in, layer, col0, w_gate, b_gate):
    m, d = h.shape
    r, kd = w_gate.shape
    assert col0 % V7X_LANES == 0 and col0 + r == w_in.shape[2] and r <= V7X_LANES
    tm = _pick(m, (512, 256, 128, 64, 32, 16))
    return pl.pallas_call(
        _gla_gate_kernel,
        grid=(m // tm,),
        in_specs=[pl.BlockSpec((tm, d), lambda i: (i, 0)),
                  pl.BlockSpec((None, d, V7X_LANES), lambda i: (layer, 0, col0 // V7X_LANES)),
                  pl.BlockSpec((r, kd), lambda i: (0, 0)),
                  pl.BlockSpec((1, kd), lambda i: (0, 0))],
        out_specs=pl.BlockSpec((tm, kd), lambda i: (i, 0)),
        out_shape=jax.ShapeDtypeStruct((m, kd), F32),
        compiler_params=_params("parallel"),
        name="gla_gate",
    )(h, w_in, w_gate, b_gate.reshape(1, kd))


def _block_mean_stream(pt_ref, cache_hbm, km_ref, pbuf, sems, *, layer, step, n_steps, pages_per_block):
    per_step, page = pbuf.shape[1:3]
    table_pages = pt_ref.shape[1]
    cur = lax.rem(step, 2)

    def page_copy(s, half, a):
        flat = s * per_step + a
        phys = pt_ref[flat // table_pages, lax.rem(flat, table_pages)]
        return pltpu.make_async_copy(cache_hbm.at[layer, phys], pbuf.at[half, a], sems.at[half, a])

    def start_fetch(s, half):
        for a in range(per_step):
            page_copy(s, half, a).start()

    @pl.when(step == 0)
    def _():
        start_fetch(step, cur)

    @pl.when(step + 1 < n_steps)
    def _():
        start_fetch(step + 1, 1 - cur)

    for a in range(per_step):
        page_copy(step, cur, a).wait()

    for a0 in range(0, per_step, pages_per_block):
        total = functools.reduce(jnp.add, [jnp.sum(pbuf[cur, a], axis=0)
                                           for a in range(a0, a0 + pages_per_block)])
        flat = step * per_step + a0
        block = lax.rem(flat, table_pages) // pages_per_block
        km_ref[flat // table_pages, block] = total * (1.0 / (pages_per_block * page))


def _gla_kernel(*refs, chunk, sub, n_valid, has_state, side):
    refs = list(refs)
    pt_ref = refs.pop(0) if side else None
    q_ref, k_ref, v_ref, g_ref, r_ref, hn_ref = refs[:6]
    del refs[:6]
    s0_ref = refs.pop(0) if has_state else None
    cache_hbm = refs.pop(0) if side else None
    o_ref, s_ref = refs[:2]
    del refs[:2]
    km_ref = refs.pop(0) if side else None
    st_ref = refs.pop(0)
    ci = pl.program_id(2)
    span, dk = q_ref.shape
    half = sub // 2

    if side:
        pbuf, sems = refs
        step = (pl.program_id(0) * pl.num_programs(1) + pl.program_id(1)) * pl.num_programs(2) + ci
        n_steps = pl.num_programs(0) * pl.num_programs(1) * pl.num_programs(2)
        _block_mean_stream(pt_ref, cache_hbm, km_ref, pbuf, sems, layer=side[0], step=step,
                           n_steps=n_steps, pages_per_block=side[1])

    @pl.when(ci == 0)
    def _():
        if s0_ref is None:
            st_ref[...] = jnp.zeros(st_ref.shape, st_ref.dtype)
        else:
            st_ref[...] = s0_ref[0, 0].T

    row = lax.broadcasted_iota(jnp.int32, (chunk, chunk), 0)
    col = lax.broadcasted_iota(jnp.int32, (chunk, chunk), 1)
    tri = jnp.where(col <= row, 1.0, 0.0).astype(BF16)
    krow = lax.broadcasted_iota(jnp.int32, (chunk, 1), 0)
    hrow = lax.broadcasted_iota(jnp.int32, (half, 1), 0)
    hcol = lax.broadcasted_iota(jnp.int32, (half, chunk), 1)
    hn = hn_ref[...]

    def within_chunk(c0):
        q = q_ref[c0:c0 + chunk, :] * (dk ** -0.5)
        k = k_ref[c0:c0 + chunk, :]
        v = v_ref[c0:c0 + chunk, :].astype(BF16)
        tok = krow + (ci * span + c0)
        g = jnp.where(tok < n_valid, g_ref[c0:c0 + chunk, :], 0.0) * LOG2_E
        g_hi, g_mid, g_lo = _split3(g)
        b = _dot(tri, g_hi) + _dot(tri, g_mid) + _dot(tri, g_lo)
        b_last = b[chunk - 1:chunk, :]

        blocks = []
        for i in range(chunk // sub):
            lo = i * sub
            bi, qi, ki = b[lo:lo + sub], q[lo:lo + sub], k[lo:lo + sub]
            for hh in range(2):
                bh, qh = bi[hh * half:(hh + 1) * half], qi[hh * half:(hh + 1) * half]
                a = jnp.zeros((half, chunk), F32)
                for jj in range((hh + 1) * half):
                    diff = bh - bi[jj:jj + 1]
                    if jj >= hh * half:
                        diff = jnp.where(hrow >= jj - hh * half, diff, NEG_INF)
                    pair = jnp.sum(qh * jnp.exp2(diff) * ki[jj:jj + 1], axis=-1, keepdims=True)
                    a = jnp.where(hcol == lo + jj, pair, a)
                blocks.append(a)
            if i > 0:
                ref_b = bi[0:1]
                qs = (qi * jnp.exp2(bi - ref_b)).astype(BF16)
                ks = (k * jnp.exp2(jnp.where(krow < lo, ref_b - b, NEG_INF))).astype(BF16)
                far = _dot_nt(qs, ks)
                blocks[-2] = blocks[-2] + far[:half]
                blocks[-1] = blocks[-1] + far[half:]
        attn = jnp.concatenate(blocks, axis=0)
        o_intra = _dot(attn.astype(BF16), v)
        q_dec = (q * jnp.exp2(b)).astype(BF16)
        k_dec = (k * jnp.exp2(b_last - b)).astype(BF16)
        r = r_ref[c0:c0 + chunk, :]
        return o_intra, q_dec, _dot_tn(v, k_dec), jnp.exp2(b_last), r * jax.nn.sigmoid(r)

    def through_state(c0, o_intra, q_dec, kv, decay, out_gate):
        st = st_ref[...]
        o = o_intra + _dot_nt(q_dec, st.astype(BF16))
        st_ref[...] = st * decay + kv
        ms = jnp.mean(o * o, axis=-1, keepdims=True)
        y = o * lax.rsqrt(ms + NORM_EPS) * hn
        o_ref[c0:c0 + chunk, :] = (y * out_gate).astype(o_ref.dtype)

    starts = list(range(0, span, chunk))
    ready = within_chunk(starts[0])
    for n, c0 in enumerate(starts):
        upcoming = within_chunk(starts[n + 1]) if n + 1 < len(starts) else None
        through_state(c0, *ready)
        ready = upcoming

    @pl.when(ci == pl.num_programs(2) - 1)
    def _():
        s_ref[0, 0] = st_ref[...].T


def _gla_core(proj, g, head_norm, state, layer, dims, seq, n_valid, block_means_of=None):
    m = proj.shape[0]
    bsz, heads, dk, dv = dims
    chunk = GLA_CHUNK
    assert seq % chunk == 0 and chunk % GLA_SUB == 0 and GLA_SUB % 16 == 0 and (2 * heads * dk) % dv == 0
    span = chunk * _pick(seq // chunk, (GLA_CHUNKS_PER_STEP, 2, 1))
    nc = seq // span
    kb0 = heads
    vb0 = 2 * heads * dk // dv
    rb0 = vb0 + heads

    def rows(bi, hi, ci):
        return bi * nc + ci

    in_specs = [pl.BlockSpec((span, dk), lambda bi, hi, ci, *_: (rows(bi, hi, ci), hi)),
                pl.BlockSpec((span, dk), lambda bi, hi, ci, *_: (rows(bi, hi, ci), kb0 + hi)),
                pl.BlockSpec((span, dv), lambda bi, hi, ci, *_: (rows(bi, hi, ci), vb0 + hi)),
                pl.BlockSpec((span, dk), lambda bi, hi, ci, *_: (rows(bi, hi, ci), hi)),
                pl.BlockSpec((span, dv), lambda bi, hi, ci, *_: (rows(bi, hi, ci), rb0 + hi)),
                pl.BlockSpec((1, dv), lambda bi, hi, ci, *_: (0, 0))]
    operands = [proj, proj, proj, g, proj, head_norm.reshape(1, dv)]
    out_specs = [pl.BlockSpec((span, dv), lambda bi, hi, ci, *_: (rows(bi, hi, ci), hi)),
                 pl.BlockSpec((1, 1, dk, dv), lambda bi, hi, ci, *_: (bi, hi, 0, 0))]
    out_shape = [jax.ShapeDtypeStruct((m, heads * dv), BF16),
                 jax.ShapeDtypeStruct((bsz, heads, dk, dv), F32)]
    scratch = [pltpu.VMEM((dv, dk), F32)]
    prefetch = []
    if state is not None:
        in_specs.append(pl.BlockSpec((None, 1, 1, dk, dv), lambda bi, hi, ci, *_: (layer, bi, hi, 0, 0)))
        operands.append(state)
    side = None
    if block_means_of is not None:
        cache, page_table = block_means_of
        _, _, page, c_heads, c_hd = cache.shape
        c_bsz, table_pages = page_table.shape
        ppb = MOBA_BLOCK // page
        n_steps = bsz * heads * nc
        per_step = c_bsz * table_pages // n_steps
        assert per_step * n_steps == c_bsz * table_pages and per_step % ppb == 0 and table_pages % per_step == 0
        side = (layer, ppb)
        prefetch = [page_table]
        in_specs.append(pl.BlockSpec(memory_space=pl.ANY))
        operands.append(cache)
        km_shape = (c_bsz, table_pages // ppb, c_heads, c_hd)
        out_specs.append(pl.BlockSpec(km_shape, lambda bi, hi, ci, *_: (0, 0, 0, 0)))
        out_shape.append(jax.ShapeDtypeStruct(km_shape, F32))
        scratch += [pltpu.VMEM((2, per_step, page, c_heads, c_hd), F32), pltpu.SemaphoreType.DMA((2, per_step))]
    grid_spec = pltpu.PrefetchScalarGridSpec(
        num_scalar_prefetch=len(prefetch), grid=(bsz, heads, nc),
        in_specs=in_specs, out_specs=out_specs, scratch_shapes=scratch)
    return pl.pallas_call(
        functools.partial(_gla_kernel, chunk=chunk, sub=GLA_SUB, n_valid=n_valid,
                          has_state=state is not None, side=side),
        grid_spec=grid_spec,
        out_shape=out_shape,
        compiler_params=_params("arbitrary", "arbitrary", "arbitrary"),
        name="gla_core",
    )(*prefetch, *operands)


def _moba_prompt_kernel(q_ref, k_ref, v_ref, o_ref, kb_ref, vt_ref, km_ref, *, blk, top_k):
    seq, hd = q_ref.shape
    nb = seq // blk
    score_scale = (hd ** -0.5) * LOG2_E

    for n in range(nb):
        kblk = k_ref[n * blk:(n + 1) * blk, :]
        km_ref[n:n + 1, :] = jnp.mean(kblk, axis=0, keepdims=True)
        kb_ref[n * blk:(n + 1) * blk, :] = kblk.astype(BF16)
        vt_ref[:, n * blk:(n + 1) * blk] = v_ref[n * blk:(n + 1) * blk, :].T.astype(BF16)
    km_hi, km_lo = _split2(km_ref[...])

    brow = lax.broadcasted_iota(jnp.int32, (nb, blk), 0)
    key = lax.broadcasted_iota(jnp.int32, (blk, blk), 0)
    qry = lax.broadcasted_iota(jnp.int32, (blk, blk), 1)
    causal = jnp.where(key <= qry, 0.0, NEG_INF)

    def scores(i):
        q = q_ref[i * blk:(i + 1) * blk, :]
        s = _dot_nt(kb_ref[0:(i + 1) * blk, :], q.astype(BF16)) * score_scale
        pieces = []
        if i > 0:
            q_hi, q_lo = _split2(q)
            gate = _dot_nt(km_hi, q_hi) + _dot_nt(km_lo, q_hi) + _dot_nt(km_hi, q_lo)
            past = brow < i
            gate = jnp.where(past, gate, NEG_INF)
            rank = jnp.zeros((nb, blk), jnp.int32)
            for n in range(i):
                gn = gate[n:n + 1, :]
                rank = rank + jnp.where(gn > gate, 1, jnp.where(gn == gate, jnp.where(brow > n, 1, 0), 0))
            bias = jnp.where(past, jnp.where(rank < top_k, 0.0, NEG_INF), NEG_INF)
            pieces = [s[n * blk:(n + 1) * blk] + bias[n:n + 1] for n in range(i)]
        return pieces + [s[i * blk:(i + 1) * blk] + causal]

    def softmax(pieces):
        m = functools.reduce(jnp.maximum, [jnp.max(p, axis=0, keepdims=True) for p in pieces])
        probs = [jnp.exp2(p - m) for p in pieces]
        l = functools.reduce(jnp.add, [jnp.sum(p, axis=0, keepdims=True) for p in probs])
        return jnp.concatenate([p.astype(BF16) for p in probs], axis=0), l

    def attend(i, probs, l):
        acc = _dot(vt_ref[:, 0:(i + 1) * blk], probs)
        o_ref[i * blk:(i + 1) * blk, :] = (acc / l).T.astype(o_ref.dtype)

    pending = scores(0)
    for i in range(nb):
        upcoming = scores(i + 1) if i + 1 < nb else None
        probs, l = softmax(pending)
        attend(i, probs, l)
        pending = upcoming


def _moba_prompt(q, k, v, bsz, seq, heads):
    m, d = q.shape
    hd = d // heads
    blk = MOBA_BLOCK
    assert seq % blk == 0 and hd == V7X_LANES
    nb = seq // blk
    spec = pl.BlockSpec((seq, hd), lambda bi, hi: (bi, hi))
    return pl.pallas_call(
        functools.partial(_moba_prompt_kernel, blk=blk, top_k=MOBA_TOP_K),
        grid=(bsz, heads),
        in_specs=[spec, spec, spec],
        out_specs=spec,
        out_shape=jax.ShapeDtypeStruct((m, d), BF16),
        scratch_shapes=[pltpu.VMEM((seq, hd), BF16),
                        pltpu.VMEM((hd, seq), BF16),
                        pltpu.VMEM((nb, hd), F32)],
        compiler_params=_params("parallel", "parallel"),
        name="moba_prompt",
    )(q, k, v)


DECODE_Q_ROWS = 16


def _moba_select_kernel(q_ref, km_ref, o_ref, *, top_k, n_q):
    km = km_ref[0]
    nbp, heads, _ = km.shape
    bidx = lax.broadcasted_iota(jnp.int32, (nbp, heads, 1), 0)
    for t in range(n_q):
        gate = jnp.sum(km * q_ref[0, t][None], axis=-1, keepdims=True)
        for r in range(top_k):
            best = jnp.max(gate, axis=0, keepdims=True)
            idx = jnp.min(jnp.where(gate == best, bidx, nbp), axis=0, keepdims=True)
            o_ref[0, t * top_k + r] = jnp.broadcast_to(idx[0], o_ref.shape[2:])
            gate = jnp.where(bidx == idx, NEG_INF, gate)


def _moba_select(q, kmean, n_q):
    bsz, _, heads, hd = q.shape
    nbp = kmean.shape[1]
    slots = n_q * MOBA_TOP_K
    sel = pl.pallas_call(
        functools.partial(_moba_select_kernel, top_k=MOBA_TOP_K, n_q=n_q),
        grid=(bsz,),
        in_specs=[pl.BlockSpec((1, n_q, heads, hd), lambda bi: (bi, 0, 0, 0)),
                  pl.BlockSpec((1, nbp, heads, hd), lambda bi: (bi, 0, 0, 0))],
        out_specs=pl.BlockSpec((1, slots, heads, V7X_LANES), lambda bi: (bi, 0, 0, 0)),
        out_shape=jax.ShapeDtypeStruct((bsz, slots, heads, V7X_LANES), jnp.int32),
        compiler_params=_params("parallel"),
        name="moba_select",
    )(q, kmean)
    return sel[..., 0]


def _moba_decode_kernel(pt_ref, sel_ref, q_ref, kn_ref, vn_ref, kc_hbm, vc_hbm, o_ref,
                        kbuf, vbuf, sems, *, layer, heads, top_k, pages_per_block, n_q):
    b = pl.program_id(0)
    h = pl.program_id(1)
    step = b * heads + h
    n_steps = pl.num_programs(0) * heads
    cur = lax.rem(step, 2)
    n_slots = n_q * top_k
    n_pages = n_slots * pages_per_block
    rows, hd = q_ref.shape[1:]
    page = kn_ref.shape[1]
    scale = hd ** -0.5
    per_query = top_k * pages_per_block

    def page_copies(bb, hh, half, i):
        slot, pg = divmod(i, pages_per_block)
        t, j = divmod(i, per_query)
        blk = sel_ref[(bb * n_slots + slot) * heads + hh]
        phys = pt_ref[bb, blk * pages_per_block + pg]
        dst = pl.ds(j * page, page)
        return (pltpu.make_async_copy(kc_hbm.at[layer, phys, :, hh, :], kbuf.at[half, t, dst], sems.at[0, half, i]),
                pltpu.make_async_copy(vc_hbm.at[layer, phys, :, hh, :], vbuf.at[half, t, dst], sems.at[1, half, i]))

    def start_gather(bb, hh, half):
        for i in range(n_pages):
            for c in page_copies(bb, hh, half, i):
                c.start()

    @pl.when(step == 0)
    def _():
        start_gather(b, h, cur)

    @pl.when(step + 1 < n_steps)
    def _():
        nxt = step + 1
        start_gather(nxt // heads, lax.rem(nxt, heads), 1 - cur)

    for i in range(n_pages):
        for c in page_copies(b, h, cur, i):
            c.wait()

    qb = q_ref[0].astype(BF16)
    lane = lax.broadcasted_iota(jnp.int32, (1, page), 1)
    rowi = lax.broadcasted_iota(jnp.int32, (rows, 1), 0)
    s_own = _dot_nt(qb, kn_ref[0].astype(BF16)) * scale
    v_own = vn_ref[0].astype(BF16)
    s_past = [_dot_nt(qb, kbuf[cur, t].astype(BF16)) * scale for t in range(n_q)]
    s_new = [jnp.where(lane <= t, s_own, NEG_INF) for t in range(n_q)]
    m = [jnp.maximum(jnp.max(sp, axis=1, keepdims=True), jnp.max(sn, axis=1, keepdims=True))
         for sp, sn in zip(s_past, s_new)]
    p_past = [jnp.exp(sp - mt) for sp, mt in zip(s_past, m)]
    p_new = [jnp.exp(sn - mt) for sn, mt in zip(s_new, m)]
    l = [jnp.sum(pp, axis=1, keepdims=True) + jnp.sum(pn, axis=1, keepdims=True)
         for pp, pn in zip(p_past, p_new)]
    acc = [_dot(pp.astype(BF16), vbuf[cur, t].astype(BF16)) + _dot(pn.astype(BF16), v_own)
           for t, (pp, pn) in enumerate(zip(p_past, p_new))]
    out = jnp.zeros((rows, hd), F32)
    for t in range(n_q):
        out = jnp.where(rowi == t, acc[t] / l[t], out)
    o_ref[0] = out


def _moba_decode(q, k_new, v_new, cache_k, cache_v, page_table, sel, layer, n_q):
    bsz, rows, d = q.shape
    _, _, page, heads, hd = cache_k.shape
    ppb = MOBA_BLOCK // page
    top_k = MOBA_TOP_K
    n_pages = n_q * top_k * ppb
    assert n_q <= rows and k_new.shape[1] == page
    q_spec = pl.BlockSpec((1, rows, hd), lambda bi, hi, pt, sl: (bi, 0, hi))
    new_spec = pl.BlockSpec((1, page, hd), lambda bi, hi, pt, sl: (bi, 0, hi))
    any_spec = pl.BlockSpec(memory_space=pl.ANY)
    grid_spec = pltpu.PrefetchScalarGridSpec(
        num_scalar_prefetch=2,
        grid=(bsz, heads),
        in_specs=[q_spec, new_spec, new_spec, any_spec, any_spec],
        out_specs=q_spec,
        scratch_shapes=[pltpu.VMEM((2, n_q, top_k * ppb * page, hd), F32),
                        pltpu.VMEM((2, n_q, top_k * ppb * page, hd), F32),
                        pltpu.SemaphoreType.DMA((2, 2, n_pages))],
    )
    return pl.pallas_call(
        functools.partial(_moba_decode_kernel, layer=layer, heads=heads, top_k=top_k,
                          pages_per_block=ppb, n_q=n_q),
        grid_spec=grid_spec,
        out_shape=jax.ShapeDtypeStruct((bsz, rows, d), F32),
        compiler_params=_params("arbitrary", "arbitrary"),
        name="moba_decode",
    )(page_table, sel.reshape(-1), q, k_new, v_new, cache_k, cache_v)


def kernel(x_prompt, x_sample, state_gla, cache_k, cache_v, page_table, norm_mix, norm_ffn, norm_final,
           gla_w_in, gla_w_gate, gla_b_gate, gla_head_norm, gla_w_out, moba_w_qkv, moba_w_o,
           ffn_w_gate_up, ffn_w_down):
    bp, seq, d = x_prompt.shape
    bd, sd, _ = x_sample.shape
    depth = norm_mix.shape[0]
    n_moba, n_pool, page, mh, mhd = cache_k.shape
    past_len = page_table.shape[1] * page
    assert past_len % MOBA_BLOCK == 0 and sd <= min(MOBA_BLOCK, 8) and GLA_CHUNK % sd == 0
    gh, gdk, gdv = state_gla.shape[2:]

    xp = x_prompt.reshape(bp * seq, d)
    xs = x_sample.reshape(bd * sd, d)
    rope_p = _rope_tables(seq, 0, seq, mhd)
    rope_s = _rope_tables(bd * sd, past_len, sd, mhd)
    kmean = {}
    pad_s = GLA_CHUNK - sd

    def pad_tokens(a, n):
        c = a.shape[1]
        return jnp.pad(a.reshape(bd, sd, c), ((0, 0), (0, n), (0, 0))).reshape(bd * (sd + n), c)

    gla_sp, gla_ss, kp_l, vp_l, ks_l, vs_l = [], [], [], [], [], []
    for i in range(depth):
        j = i // 2
        hp = _rmsnorm(xp, norm_mix[i], BF16)
        hs = _rmsnorm(xs, norm_mix[i], BF16)
        if i % 2 == 0:
            n_main = 2 * gh * gdk + 2 * gh * gdv
            proj_p, proj_s = _matmul(hp, hs, gla_w_in, j, n_main)
            g_p = _gla_gate(hp, gla_w_in, j, n_main, gla_w_gate[j], gla_b_gate[j])
            g_s = _gla_gate(hs, gla_w_in, j, n_main, gla_w_gate[j], gla_b_gate[j])
            assert i + 1 < depth and j < n_moba
            op, sp, kmean[j] = _gla_core(proj_p, g_p, gla_head_norm[j], None, j, (bp, gh, gdk, gdv), seq, seq,
                                         block_means_of=(cache_k, page_table))
            os_pad, ss = _gla_core(pad_tokens(proj_s, pad_s), pad_tokens(g_s, pad_s), gla_head_norm[j],
                                   state_gla, j, (bd, gh, gdk, gdv), sd + pad_s, sd)
            os_ = os_pad.reshape(bd, sd + pad_s, d)[:, :sd].reshape(bd * sd, d)
            xp, xs, ssq_p, ssq_s = _matmul(op, os_, gla_w_out, j, d, res=(xp, xs), with_ssq=True)
            gla_sp.append(sp)
            gla_ss.append(ss)
        else:
            rope = (rope_p, rope_s)
            qp, qs = _matmul(hp, hs, moba_w_qkv, j, d, 0, rope=rope)
            kp, ks = _matmul(hp, hs, moba_w_qkv, j, d, d, rope=rope)
            vp, vs = _matmul(hp, hs, moba_w_qkv, j, d, 2 * d)
            op = _moba_prompt(qp, kp, vp, bp, seq, mh)
            q3 = pad_tokens(qs, DECODE_Q_ROWS - sd).reshape(bd, DECODE_Q_ROWS, d)
            k3, v3 = (pad_tokens(a, page - sd).reshape(bd, page, d) for a in (ks, vs))
            sel = _moba_select(qs.reshape(bd, sd, mh, mhd), kmean[j], sd)
            o3 = _moba_decode(q3, k3, v3, cache_k, cache_v, page_table, sel, j, sd)
            xp, xs, ssq_p, ssq_s = _matmul(op, o3[:, :sd].reshape(bd * sd, d), moba_w_o, j, d,
                                           res=(xp, xs), with_ssq=True)
            kp_l.append(kp)
            vp_l.append(vp)
            ks_l.append(ks)
            vs_l.append(vs)
        xp, xs = _ffn(xp, xs, ssq_p, ssq_s, norm_ffn[i], ffn_w_gate_up, ffn_w_down, i)

    y_prompt = _rmsnorm(xp, norm_final, F32).reshape(bp, seq, d)
    y_sample = _rmsnorm(xs, norm_final, F32).reshape(bd, sd, d)
    def pages(rows_per_layer, lead):
        return jnp.stack(rows_per_layer).reshape(len(rows_per_layer), *lead, mh, mhd)

    return (y_prompt, y_sample, jnp.stack(gla_sp), jnp.stack(gla_ss),
            pages(kp_l, (bp * seq // page, page)), pages(vp_l, (bp * seq // page, page)),
            pages(ks_l, (bd, sd)), pages(vs_l, (bd, sd)))
```

```python
import functools

import jax
import jax.numpy as jnp
from jax import lax
from jax.experimental import pallas as pl
from jax.experimental.pallas import tpu as pltpu

NORM_EPS = 1e-6
GLA_HEADS = 4
GLA_GATE_RANK = 16
GLA_GATE_NORMALIZER = 16.0
GLA_CHUNK = 64
GLA_SUB = 16
GLA_CHUNKS_PER_STEP = 8
MOBA_HEADS = 16
MOBA_BLOCK = 256
MOBA_TOP_K = 3
ROPE_THETA = 500000.0
ROPE_FRACTION = 4
PAGE_SIZE = 128

V7X_LANES = 128
V7X_VMEM_LIMIT_BYTES = 56 * 1024 * 1024

F32 = jnp.float32
BF16 = jnp.bfloat16
NEG_INF = float("-inf")
LOG2_E = 1.4426950408889634


def _params(*sem):
    return pltpu.CompilerParams(dimension_semantics=sem, vmem_limit_bytes=V7X_VMEM_LIMIT_BYTES)


def _pick(n, candidates):
    for c in candidates:
        if n % c == 0:
            return c
    raise ValueError(f"no tile for {n} in {candidates}")


def _dot(a, b):
    return jnp.dot(a, b, preferred_element_type=F32)


def _dot_nt(a, b):
    return lax.dot_general(a, b, (((1,), (1,)), ((), ())), preferred_element_type=F32)


def _dot_tn(a, b):
    return lax.dot_general(a, b, (((0,), (0,)), ((), ())), preferred_element_type=F32)


def _split2(x):
    hi = x.astype(BF16)
    lo = (x - hi.astype(F32)).astype(BF16)
    return hi, lo


def _split3(x):
    hi = x.astype(BF16)
    r1 = x - hi.astype(F32)
    mid = r1.astype(BF16)
    lo = (r1 - mid.astype(F32)).astype(BF16)
    return hi, mid, lo


def _rmsnorm_kernel(x_ref, g_ref, o_ref):
    x = x_ref[...]
    ms = jnp.mean(x * x, axis=-1, keepdims=True)
    o_ref[...] = (x * lax.rsqrt(ms + NORM_EPS) * g_ref[...]).astype(o_ref.dtype)


def _rmsnorm(x, g, out_dtype):
    m, d = x.shape
    tm = _pick(m, (512, 256, 128, 64, 32, 16))
    return pl.pallas_call(
        _rmsnorm_kernel,
        grid=(m // tm,),
        in_specs=[pl.BlockSpec((tm, d), lambda i: (i, 0)),
                  pl.BlockSpec((1, d), lambda i: (0, 0))],
        out_specs=pl.BlockSpec((tm, d), lambda i: (i, 0)),
        out_shape=jax.ShapeDtypeStruct((m, d), out_dtype),
        compiler_params=_params("parallel"),
        name="rmsnorm",
    )(x, g.reshape(1, d))


def _rope(acc, cos_ref, sin_ref, rot_half):
    tn = acc.shape[1]
    reps = tn // V7X_LANES
    cos = jnp.concatenate([cos_ref[...]] * reps, axis=1)
    sin = jnp.concatenate([sin_ref[...]] * reps, axis=1)
    lane = lax.broadcasted_iota(jnp.int32, acc.shape, 1) % V7X_LANES
    partner = jnp.where(lane < rot_half,
                        pltpu.roll(acc, tn - rot_half, 1),
                        pltpu.roll(acc, rot_half, 1))
    return acc * cos + partner * sin


def _mm_kernel(*refs, kind, rot_half, with_ssq):
    n_extra = {"plain": 0, "res": 1, "rope": 2}[kind]
    x_ref, w_ref = refs[:2]
    extras = refs[2:2 + n_extra]
    xd_ref = refs[2 + n_extra]
    extras_d = refs[3 + n_extra:3 + 2 * n_extra]
    outs = refs[3 + 2 * n_extra:]
    if with_ssq:
        o_ref, od_ref, ssq_ref, ssqd_ref, wb_ref = outs
    else:
        (o_ref, od_ref, wb_ref), ssq_ref, ssqd_ref = outs, None, None

    def finish(acc, ex, out_ref, sq_ref):
        if kind == "res":
            acc = ex[0][...] + acc
        elif kind == "rope":
            acc = _rope(acc, ex[0], ex[1], rot_half)
        out_ref[...] = acc.astype(out_ref.dtype)
        if sq_ref is not None:
            sq_ref[...] = jnp.broadcast_to(jnp.sum(acc * acc, axis=-1, keepdims=True), sq_ref.shape)

    tm = x_ref.shape[0]

    @pl.when(pl.program_id(1) == 0)
    def _():
        wb_ref[...] = w_ref[...].astype(BF16)
        rows = jnp.concatenate([x_ref[...].astype(BF16), xd_ref[...].astype(BF16)], axis=0)
        acc = _dot(rows, wb_ref[...])
        finish(acc[:tm], extras, o_ref, ssq_ref)
        finish(acc[tm:], extras_d, od_ref, ssqd_ref)

    @pl.when(pl.program_id(1) > 0)
    def _():
        finish(_dot(x_ref[...].astype(BF16), wb_ref[...]), extras, o_ref, ssq_ref)


def _mm_swiglu_kernel(x_ref, ssq_ref, gain_ref, wg_ref, wu_ref, xd_ref, ssqd_ref, o_ref, od_ref, wgb_ref, wub_ref):
    tm, k = x_ref.shape

    def normed(x, ssq):
        total = functools.reduce(jnp.add, [ssq[:, c:c + 1] for c in range(0, ssq.shape[1], V7X_LANES)])
        return (x * lax.rsqrt(total * (1.0 / k) + NORM_EPS) * gain_ref[...]).astype(BF16)

    def gated(h):
        g = _dot(h, wgb_ref[...])
        u = _dot(h, wub_ref[...])
        return g * jax.nn.sigmoid(g) * u

    @pl.when(pl.program_id(1) == 0)
    def _():
        wgb_ref[...] = wg_ref[...].astype(BF16)
        wub_ref[...] = wu_ref[...].astype(BF16)
        h = jnp.concatenate([normed(x_ref[...], ssq_ref[...]), normed(xd_ref[...], ssqd_ref[...])], axis=0)
        y = gated(h)
        o_ref[...] = y[:tm].astype(o_ref.dtype)
        od_ref[...] = y[tm:].astype(od_ref.dtype)

    @pl.when(pl.program_id(1) > 0)
    def _():
        o_ref[...] = gated(normed(x_ref[...], ssq_ref[...])).astype(o_ref.dtype)


def _mm_tiles(m, k, n):
    tm = _pick(m, (1024, 512, 256, 128, 64, 32, 16))
    tn = _pick(n, (1024, 512, 256, 128))
    if k > 4096:
        tm, tn = min(tm, 512), min(tn, 512)
    return tm, tn


def _matmul(x, xd, w, layer, n, col0=0, out_dtype=F32, res=None, rope=None, with_ssq=False):
    m, k = x.shape
    md = xd.shape[0]
    tm, tn = _mm_tiles(m, k, n)
    assert col0 % tn == 0 and w.shape[1] == k and xd.shape[1] == k
    cb0 = col0 // tn
    x_spec = pl.BlockSpec((tm, k), lambda j, i: (i, 0))
    xd_spec = pl.BlockSpec((md, k), lambda j, i: (0, 0))
    w_spec = pl.BlockSpec((None, k, tn), lambda j, i: (layer, 0, cb0 + j))
    o_spec = pl.BlockSpec((tm, tn), lambda j, i: (i, j))
    od_spec = pl.BlockSpec((md, tn), lambda j, i: (0, j))
    kind, rot_half, ex, ex_specs, exd, exd_specs = "plain", 0, (), [], (), []
    if res is not None:
        kind, ex, ex_specs, exd, exd_specs = "res", (res[0],), [o_spec], (res[1],), [od_spec]
    elif rope is not None:
        (cos, sin, rot_half), (cos_d, sin_d, _) = rope
        assert cos.shape[0] % tm == 0 and cos_d.shape[0] == md
        period = cos.shape[0] // tm
        kind = "rope"
        ex, ex_specs = (cos, sin), [pl.BlockSpec((tm, V7X_LANES), lambda j, i: (i % period, 0))] * 2
        exd, exd_specs = (cos_d, sin_d), [pl.BlockSpec((md, V7X_LANES), lambda j, i: (0, 0))] * 2
    out_specs = [o_spec, od_spec]
    out_shape = [jax.ShapeDtypeStruct((m, n), out_dtype), jax.ShapeDtypeStruct((md, n), out_dtype)]
    if with_ssq:
        out_specs += [pl.BlockSpec((tm, V7X_LANES), lambda j, i: (i, j)),
                      pl.BlockSpec((md, V7X_LANES), lambda j, i: (0, j))]
        out_shape += [jax.ShapeDtypeStruct((m, n // tn * V7X_LANES), F32),
                      jax.ShapeDtypeStruct((md, n // tn * V7X_LANES), F32)]
    return pl.pallas_call(
        functools.partial(_mm_kernel, kind=kind, rot_half=rot_half, with_ssq=with_ssq),
        grid=(n // tn, m // tm),
        in_specs=[x_spec, w_spec, *ex_specs, xd_spec, *exd_specs],
        out_specs=out_specs,
        out_shape=out_shape,
        scratch_shapes=[pltpu.VMEM((k, tn), BF16)],
        compiler_params=_params("arbitrary", "arbitrary"),
        name="matmul",
    )(x, w, *ex, xd, *exd)


def _swiglu_up(x, xd, ssq, ssq_d, gain, w_gate_up, layer):
    m, k = x.shape
    md = xd.shape[0]
    f = w_gate_up.shape[2] // 2
    tm, tn = _mm_tiles(m, k, f)
    nb = f // tn
    groups = ssq.shape[1]
    return pl.pallas_call(
        _mm_swiglu_kernel,
        grid=(nb, m // tm),
        in_specs=[pl.BlockSpec((tm, k), lambda j, i: (i, 0)),
                  pl.BlockSpec((tm, groups), lambda j, i: (i, 0)),
                  pl.BlockSpec((1, k), lambda j, i: (0, 0)),
                  pl.BlockSpec((None, k, tn), lambda j, i: (layer, 0, j)),
                  pl.BlockSpec((None, k, tn), lambda j, i: (layer, 0, nb + j)),
                  pl.BlockSpec((md, k), lambda j, i: (0, 0)),
                  pl.BlockSpec((md, groups), lambda j, i: (0, 0))],
        out_specs=[pl.BlockSpec((tm, tn), lambda j, i: (i, j)),
                   pl.BlockSpec((md, tn), lambda j, i: (0, j))],
        out_shape=[jax.ShapeDtypeStruct((m, f), BF16), jax.ShapeDtypeStruct((md, f), BF16)],
        scratch_shapes=[pltpu.VMEM((k, tn), BF16), pltpu.VMEM((k, tn), BF16)],
        compiler_params=_params("arbitrary", "arbitrary"),
        name="swiglu_up",
    )(x, ssq, gain.reshape(1, k), w_gate_up, w_gate_up, xd, ssq_d)


def _ffn(x, xd, ssq, ssq_d, g, w_gate_up, w_down, layer):
    hid, hid_d = _swiglu_up(x, xd, ssq, ssq_d, g, w_gate_up, layer)
    return _matmul(hid, hid_d, w_down, layer, w_down.shape[2], res=(x, xd))


def _rope_table_kernel(inv_ref, cos_ref, sin_ref, *, base, period, rot_half):
    rows = cos_ref.shape[0]
    row = lax.broadcasted_iota(jnp.int32, (rows, V7X_LANES), 0) + pl.program_id(0) * rows
    lane = lax.broadcasted_iota(jnp.int32, (rows, V7X_LANES), 1)
    pos = (base + (row & (period - 1))).astype(F32)
    ang = pos * inv_ref[...]
    c, s = jnp.cos(ang), jnp.sin(ang)
    cos_ref[...] = jnp.where(lane < 2 * rot_half, c, 1.0)
    sin_ref[...] = jnp.where(lane < rot_half, -s, jnp.where(lane < 2 * rot_half, s, 0.0))


def _rope_tables(rows, base, period, head_dim):
    assert period & (period - 1) == 0 and head_dim == V7X_LANES
    rot_half = head_dim // ROPE_FRACTION // 2
    inv = ROPE_THETA ** (-jnp.arange(rot_half, dtype=F32) / rot_half)
    inv = jnp.concatenate([inv, inv, jnp.zeros((head_dim - 2 * rot_half,), F32)]).reshape(1, head_dim)
    tr = _pick(rows, (256, 128, 64, 32, 16, 8))
    spec = pl.BlockSpec((tr, head_dim), lambda i: (i, 0))
    cos, sin = pl.pallas_call(
        functools.partial(_rope_table_kernel, base=base, period=period, rot_half=rot_half),
        grid=(rows // tr,),
        in_specs=[pl.BlockSpec((1, head_dim), lambda i: (0, 0))],
        out_specs=[spec, spec],
        out_shape=[jax.ShapeDtypeStruct((rows, head_dim), F32)] * 2,
        compiler_params=_params("parallel"),
        name="rope_tables",
    )(inv)
    return cos, sin, rot_half


def _gla_gate_kernel(h_ref, wlr_ref, wg_ref, b_ref, o_ref):
    rank = wg_ref.shape[0]
    glr = _dot(h_ref[...], wlr_ref[:, :rank].astype(BF16))
    z = _dot(glr.astype(BF16), wg_ref[...].astype(BF16)) + b_ref[...]
    o_ref[...] = (jnp.minimum(z, 0.0) - jnp.log1p(jnp.exp(-jnp.abs(z)))) * (1.0 / GLA_GATE_NORMALIZER)


def _gla_gate(h, w_in, layer, col0, w_gate, b_gate):
    m, d = h.shape
    r, kd = w_gate.shape
    assert col0 % V7X_LANES == 0 and col0 + r == w_in.shape[2] and r <= V7X_LANES
    tm = _pick(m, (512, 256, 128, 64, 32, 16))
    return pl.pallas_call(
        _gla_gate_kernel,
        grid=(m // tm,),
        in_specs=[pl.BlockSpec((tm, d), lambda i: (i, 0)),
                  pl.BlockSpec((None, d, V7X_LANES), lambda i: (layer, 0, col0 // V7X_LANES)),
                  pl.BlockSpec((r, kd), lambda i: (0, 0)),
                  pl.BlockSpec((1, kd), lambda i: (0, 0))],
        out_specs=pl.BlockSpec((tm, kd), lambda i: (i, 0)),
        out_shape=jax.ShapeDtypeStruct((m, kd), F32),
        compiler_params=_params("parallel"),
        name="gla_gate",
    )(h, w_in, w_gate, b_gate.reshape(1, kd))


def _block_mean_stream(pt_ref, cache_hbm, km_ref, pbuf, sems, *, layer, step, n_steps, pages_per_block):
    per_step, page = pbuf.shape[1:3]
    table_pages = pt_ref.shape[1]
    cur = lax.rem(step, 2)

    def page_copy(s, half, a):
        flat = s * per_step + a
        phys = pt_ref[flat // table_pages, lax.rem(flat, table_pages)]
        return pltpu.make_async_copy(cache_hbm.at[layer, phys], pbuf.at[half, a], sems.at[half, a])

    def start_fetch(s, half):
        for a in range(per_step):
            page_copy(s, half, a).start()

    @pl.when(step == 0)
    def _():
        start_fetch(step, cur)

    @pl.when(step + 1 < n_steps)
    def _():
        start_fetch(step + 1, 1 - cur)

    for a in range(per_step):
        page_copy(step, cur, a).wait()

    for a0 in range(0, per_step, pages_per_block):
        total = functools.reduce(jnp.add, [jnp.sum(pbuf[cur, a], axis=0)
                                           for a in range(a0, a0 + pages_per_block)])
        flat = step * per_step + a0
        block = lax.rem(flat, table_pages) // pages_per_block
        km_ref[flat // table_pages, block] = total * (1.0 / (pages_per_block * page))


def _gla_kernel(*refs, chunk, sub, n_valid, has_state, side):
    refs = list(refs)
    pt_ref = refs.pop(0) if side else None
    q_ref, k_ref, v_ref, g_ref, r_ref, hn_ref = refs[:6]
    del refs[:6]
    s0_ref = refs.pop(0) if has_state else None
    cache_hbm = refs.pop(0) if side else None
    o_ref, s_ref = refs[:2]
    del refs[:2]
    km_ref = refs.pop(0) if side else None
    st_ref = refs.pop(0)
    ci = pl.program_id(2)
    span, dk = q_ref.shape
    half = sub // 2

    if side:
        pbuf, sems = refs
        step = (pl.program_id(0) * pl.num_programs(1) + pl.program_id(1)) * pl.num_programs(2) + ci
        n_steps = pl.num_programs(0) * pl.num_programs(1) * pl.num_programs(2)
        _block_mean_stream(pt_ref, cache_hbm, km_ref, pbuf, sems, layer=side[0], step=step,
                           n_steps=n_steps, pages_per_block=side[1])

    @pl.when(ci == 0)
    def _():
        if s0_ref is None:
            st_ref[...] = jnp.zeros(st_ref.shape, st_ref.dtype)
        else:
            st_ref[...] = s0_ref[0, 0].T

    row = lax.broadcasted_iota(jnp.int32, (chunk, chunk), 0)
    col = lax.broadcasted_iota(jnp.int32, (chunk, chunk), 1)
    tri = jnp.where(col <= row, 1.0, 0.0).astype(BF16)
    krow = lax.broadcasted_iota(jnp.int32, (chunk, 1), 0)
    hrow = lax.broadcasted_iota(jnp.int32, (half, 1), 0)
    hcol = lax.broadcasted_iota(jnp.int32, (half, chunk), 1)
    hn = hn_ref[...]

    def within_chunk(c0):
        q = q_ref[c0:c0 + chunk, :] * (dk ** -0.5)
        k = k_ref[c0:c0 + chunk, :]
        v = v_ref[c0:c0 + chunk, :].astype(BF16)
        tok = krow + (ci * span + c0)
        g = jnp.where(tok < n_valid, g_ref[c0:c0 + chunk, :], 0.0) * LOG2_E
        g_hi, g_mid, g_lo = _split3(g)
        b = _dot(tri, g_hi) + _dot(tri, g_mid) + _dot(tri, g_lo)
        b_last = b[chunk - 1:chunk, :]

        blocks = []
        for i in range(chunk // sub):
            lo = i * sub
            bi, qi, ki = b[lo:lo + sub], q[lo:lo + sub], k[lo:lo + sub]
            for hh in range(2):
                bh, qh = bi[hh * half:(hh + 1) * half], qi[hh * half:(hh + 1) * half]
                a = jnp.zeros((half, chunk), F32)
                for jj in range((hh + 1) * half):
                    diff = bh - bi[jj:jj + 1]
                    if jj >= hh * half:
                        diff = jnp.where(hrow >= jj - hh * half, diff, NEG_INF)
                    pair = jnp.sum(qh * jnp.exp2(diff) * ki[jj:jj + 1], axis=-1, keepdims=True)
                    a = jnp.where(hcol == lo + jj, pair, a)
                blocks.append(a)
            if i > 0:
                ref_b = bi[0:1]
                qs = (qi * jnp.exp2(bi - ref_b)).astype(BF16)
                ks = (k * jnp.exp2(jnp.where(krow < lo, ref_b - b, NEG_INF))).astype(BF16)
                far = _dot_nt(qs, ks)
                blocks[-2] = blocks[-2] + far[:half]
                blocks[-1] = blocks[-1] + far[half:]
        attn = jnp.concatenate(blocks, axis=0)
        o_intra = _dot(attn.astype(BF16), v)
        q_dec = (q * jnp.exp2(b)).astype(BF16)
        k_dec = (k * jnp.exp2(b_last - b)).astype(BF16)
        r = r_ref[c0:c0 + chunk, :]
        return o_intra, q_dec, _dot_tn(v, k_dec), jnp.exp2(b_last), r * jax.nn.sigmoid(r)

    def through_state(c0, o_intra, q_dec, kv, decay, out_gate):
        st = st_ref[...]
        o = o_intra + _dot_nt(q_dec, st.astype(BF16))
        st_ref[...] = st * decay + kv
        ms = jnp.mean(o * o, axis=-1, keepdims=True)
        y = o * lax.rsqrt(ms + NORM_EPS) * hn
        o_ref[c0:c0 + chunk, :] = (y * out_gate).astype(o_ref.dtype)

    starts = list(range(0, span, chunk))
    ready = within_chunk(starts[0])
    for n, c0 in enumerate(starts):
        upcoming = within_chunk(starts[n + 1]) if n + 1 < len(starts) else None
        through_state(c0, *ready)
        ready = upcoming

    @pl.when(ci == pl.num_programs(2) - 1)
    def _():
        s_ref[0, 0] = st_ref[...].T


def _gla_core(proj, g, head_norm, state, layer, dims, seq, n_valid, block_means_of=None):
    m = proj.shape[0]
    bsz, heads, dk, dv = dims
    chunk = GLA_CHUNK
    assert seq % chunk == 0 and chunk % GLA_SUB == 0 and GLA_SUB % 16 == 0 and (2 * heads * dk) % dv == 0
    span = chunk * _pick(seq // chunk, (GLA_CHUNKS_PER_STEP, 2, 1))
    nc = seq // span
    kb0 = heads
    vb0 = 2 * heads * dk // dv
    rb0 = vb0 + heads

    def rows(bi, hi, ci):
        return bi * nc + ci

    in_specs = [pl.BlockSpec((span, dk), lambda bi, hi, ci, *_: (rows(bi, hi, ci), hi)),
                pl.BlockSpec((span, dk), lambda bi, hi, ci, *_: (rows(bi, hi, ci), kb0 + hi)),
                pl.BlockSpec((span, dv), lambda bi, hi, ci, *_: (rows(bi, hi, ci), vb0 + hi)),
                pl.BlockSpec((span, dk), lambda bi, hi, ci, *_: (rows(bi, hi, ci), hi)),
                pl.BlockSpec((span, dv), lambda bi, hi, ci, *_: (rows(bi, hi, ci), rb0 + hi)),
                pl.BlockSpec((1, dv), lambda bi, hi, ci, *_: (0, 0))]
    operands = [proj, proj, proj, g, proj, head_norm.reshape(1, dv)]
    out_specs = [pl.BlockSpec((span, dv), lambda bi, hi, ci, *_: (rows(bi, hi, ci), hi)),
                 pl.BlockSpec((1, 1, dk, dv), lambda bi, hi, ci, *_: (bi, hi, 0, 0))]
    out_shape = [jax.ShapeDtypeStruct((m, heads * dv), BF16),
                 jax.ShapeDtypeStruct((bsz, heads, dk, dv), F32)]
    scratch = [pltpu.VMEM((dv, dk), F32)]
    prefetch = []
    if state is not None:
        in_specs.append(pl.BlockSpec((None, 1, 1, dk, dv), lambda bi, hi, ci, *_: (layer, bi, hi, 0, 0)))
        operands.append(state)
    side = None
    if block_means_of is not None:
        cache, page_table = block_means_of
        _, _, page, c_heads, c_hd = cache.shape
        c_bsz, table_pages = page_table.shape
        ppb = MOBA_BLOCK // page
        n_steps = bsz * heads * nc
        per_step = c_bsz * table_pages // n_steps
        assert per_step * n_steps == c_bsz * table_pages and per_step % ppb == 0 and table_pages % per_step == 0
        side = (layer, ppb)
        prefetch = [page_table]
        in_specs.append(pl.BlockSpec(memory_space=pl.ANY))
        operands.append(cache)
        km_shape = (c_bsz, table_pages // ppb, c_heads, c_hd)
        out_specs.append(pl.BlockSpec(km_shape, lambda bi, hi, ci, *_: (0, 0, 0, 0)))
        out_shape.append(jax.ShapeDtypeStruct(km_shape, F32))
        scratch += [pltpu.VMEM((2, per_step, page, c_heads, c_hd), F32), pltpu.SemaphoreType.DMA((2, per_step))]
    grid_spec = pltpu.PrefetchScalarGridSpec(
        num_scalar_prefetch=len(prefetch), grid=(bsz, heads, nc),
        in_specs=in_specs, out_specs=out_specs, scratch_shapes=scratch)
    return pl.pallas_call(
        functools.partial(_gla_kernel, chunk=chunk, sub=GLA_SUB, n_valid=n_valid,
                          has_state=state is not None, side=side),
        grid_spec=grid_spec,
        out_shape=out_shape,
        compiler_params=_params("arbitrary", "arbitrary", "arbitrary"),
        name="gla_core",
    )(*prefetch, *operands)


def _moba_prompt_kernel(q_ref, k_ref, v_ref, o_ref, kb_ref, vt_ref, km_ref, *, blk, top_k):
    seq, hd = q_ref.shape
    nb = seq // blk
    score_scale = (hd ** -0.5) * LOG2_E

    for n in range(nb):
        kblk = k_ref[n * blk:(n + 1) * blk, :]
        km_ref[n:n + 1, :] = jnp.mean(kblk, axis=0, keepdims=True)
        kb_ref[n * blk:(n + 1) * blk, :] = kblk.astype(BF16)
        vt_ref[:, n * blk:(n + 1) * blk] = v_ref[n * blk:(n + 1) * blk, :].T.astype(BF16)
    km_hi, km_lo = _split2(km_ref[...])

    brow = lax.broadcasted_iota(jnp.int32, (nb, blk), 0)
    key = lax.broadcasted_iota(jnp.int32, (blk, blk), 0)
    qry = lax.broadcasted_iota(jnp.int32, (blk, blk), 1)
    causal = jnp.where(key <= qry, 0.0, NEG_INF)

    def scores(i):
        q = q_ref[i * blk:(i + 1) * blk, :]
        s = _dot_nt(kb_ref[0:(i + 1) * blk, :], q.astype(BF16)) * score_scale
        pieces = []
        if i > 0:
            q_hi, q_lo = _split2(q)
            gate = _dot_nt(km_hi, q_hi) + _dot_nt(km_lo, q_hi) + _dot_nt(km_hi, q_lo)
            past = brow < i
            gate = jnp.where(past, gate, NEG_INF)
            rank = jnp.zeros((nb, blk), jnp.int32)
            for n in range(i):
                gn = gate[n:n + 1, :]
                rank = rank + jnp.where(gn > gate, 1, jnp.where(gn == gate, jnp.where(brow > n, 1, 0), 0))
            bias = jnp.where(past, jnp.where(rank < top_k, 0.0, NEG_INF), NEG_INF)
            pieces = [s[n * blk:(n + 1) * blk] + bias[n:n + 1] for n in range(i)]
        return pieces + [s[i * blk:(i + 1) * blk] + causal]

    def softmax(pieces):
        m = functools.reduce(jnp.maximum, [jnp.max(p, axis=0, keepdims=True) for p in pieces])
        probs = [jnp.exp2(p - m) for p in pieces]
        l = functools.reduce(jnp.add, [jnp.sum(p, axis=0, keepdims=True) for p in probs])
        return jnp.concatenate([p.astype(BF16) for p in probs], axis=0), l

    def attend(i, probs, l):
        acc = _dot(vt_ref[:, 0:(i + 1) * blk], probs)
        o_ref[i * blk:(i + 1) * blk, :] = (acc / l).T.astype(o_ref.dtype)

    pending = scores(0)
    for i in range(nb):
        upcoming = scores(i + 1) if i + 1 < nb else None
        probs, l = softmax(pending)
        attend(i, probs, l)
        pending = upcoming


def _moba_prompt(q, k, v, bsz, seq, heads):
    m, d = q.shape
    hd = d // heads
    blk = MOBA_BLOCK
    assert seq % blk == 0 and hd == V7X_LANES
    nb = seq // blk
    spec = pl.BlockSpec((seq, hd), lambda bi, hi: (bi, hi))
    return pl.pallas_call(
        functools.partial(_moba_prompt_kernel, blk=blk, top_k=MOBA_TOP_K),
        grid=(bsz, heads),
        in_specs=[spec, spec, spec],
        out_specs=spec,
        out_shape=jax.ShapeDtypeStruct((m, d), BF16),
        scratch_shapes=[pltpu.VMEM((seq, hd), BF16),
                        pltpu.VMEM((hd, seq), BF16),
                        pltpu.VMEM((nb, hd), F32)],
        compiler_params=_params("parallel", "parallel"),
        name="moba_prompt",
    )(q, k, v)


DECODE_Q_ROWS = 16


def _moba_select_kernel(q_ref, km_ref, o_ref, *, top_k, n_q):
    km = km_ref[0]
    nbp, heads, _ = km.shape
    bidx = lax.broadcasted_iota(jnp.int32, (nbp, heads, 1), 0)
    for t in range(n_q):
        gate = jnp.sum(km * q_ref[0, t][None], axis=-1, keepdims=True)
        for r in range(top_k):
            best = jnp.max(gate, axis=0, keepdims=True)
            idx = jnp.min(jnp.where(gate == best, bidx, nbp), axis=0, keepdims=True)
            o_ref[0, t * top_k + r] = jnp.broadcast_to(idx[0], o_ref.shape[2:])
            gate = jnp.where(bidx == idx, NEG_INF, gate)


def _moba_select(q, kmean, n_q):
    bsz, _, heads, hd = q.shape
    nbp = kmean.shape[1]
    slots = n_q * MOBA_TOP_K
    sel = pl.pallas_call(
        functools.partial(_moba_select_kernel, top_k=MOBA_TOP_K, n_q=n_q),
        grid=(bsz,),
        in_specs=[pl.BlockSpec((1, n_q, heads, hd), lambda bi: (bi, 0, 0, 0)),
                  pl.BlockSpec((1, nbp, heads, hd), lambda bi: (bi, 0, 0, 0))],
        out_specs=pl.BlockSpec((1, slots, heads, V7X_LANES), lambda bi: (bi, 0, 0, 0)),
        out_shape=jax.ShapeDtypeStruct((bsz, slots, heads, V7X_LANES), jnp.int32),
        compiler_params=_params("parallel"),
        name="moba_select",
    )(q, kmean)
    return sel[..., 0]


def _moba_decode_kernel(pt_ref, sel_ref, q_ref, kn_ref, vn_ref, kc_hbm, vc_hbm, o_ref,
                        kbuf, vbuf, sems, *, layer, heads, top_k, pages_per_block, n_q):
    b = pl.program_id(0)
    h = pl.program_id(1)
    step = b * heads + h
    n_steps = pl.num_programs(0) * heads
    cur = lax.rem(step, 2)
    n_slots = n_q * top_k
    n_pages = n_slots * pages_per_block
    rows, hd = q_ref.shape[1:]
    page = kn_ref.shape[1]
    scale = hd ** -0.5
    per_query = top_k * pages_per_block

    def page_copies(bb, hh, half, i):
        slot, pg = divmod(i, pages_per_block)
        t, j = divmod(i, per_query)
        blk = sel_ref[(bb * n_slots + slot) * heads + hh]
        phys = pt_ref[bb, blk * pages_per_block + pg]
        dst = pl.ds(j * page, page)
        return (pltpu.make_async_copy(kc_hbm.at[layer, phys, :, hh, :], kbuf.at[half, t, dst], sems.at[0, half, i]),
                pltpu.make_async_copy(vc_hbm.at[layer, phys, :, hh, :], vbuf.at[half, t, dst], sems.at[1, half, i]))

    def start_gather(bb, hh, half):
        for i in range(n_pages):
            for c in page_copies(bb, hh, half, i):
                c.start()

    @pl.when(step == 0)
    def _():
        start_gather(b, h, cur)

    @pl.when(step + 1 < n_steps)
    def _():
        nxt = step + 1
        start_gather(nxt // heads, lax.rem(nxt, heads), 1 - cur)

    for i in range(n_pages):
        for c in page_copies(b, h, cur, i):
            c.wait()

    qb = q_ref[0].astype(BF16)
    lane = lax.broadcasted_iota(jnp.int32, (1, page), 1)
    rowi = lax.broadcasted_iota(jnp.int32, (rows, 1), 0)
    s_own = _dot_nt(qb, kn_ref[0].astype(BF16)) * scale
    v_own = vn_ref[0].astype(BF16)
    s_past = [_dot_nt(qb, kbuf[cur, t].astype(BF16)) * scale for t in range(n_q)]
    s_new = [jnp.where(lane <= t, s_own, NEG_INF) for t in range(n_q)]
    m = [jnp.maximum(jnp.max(sp, axis=1, keepdims=True), jnp.max(sn, axis=1, keepdims=True))
         for sp, sn in zip(s_past, s_new)]
    p_past = [jnp.exp(sp - mt) for sp, mt in zip(s_past, m)]
    p_new = [jnp.exp(sn - mt) for sn, mt in zip(s_new, m)]
    l = [jnp.sum(pp, axis=1, keepdims=True) + jnp.sum(pn, axis=1, keepdims=True)
         for pp, pn in zip(p_past, p_new)]
    acc = [_dot(pp.astype(BF16), vbuf[cur, t].astype(BF16)) + _dot(pn.astype(BF16), v_own)
           for t, (pp, pn) in enumerate(zip(p_past, p_new))]
    out = jnp.zeros((rows, hd), F32)
    for t in range(n_q):
        out = jnp.where(rowi == t, acc[t] / l[t], out)
    o_ref[0] = out


def _moba_decode(q, k_new, v_new, cache_k, cache_v, page_table, sel, layer, n_q):
    bsz, rows, d = q.shape
    _, _, page, heads, hd = cache_k.shape
    ppb = MOBA_BLOCK // page
    top_k = MOBA_TOP_K
    n_pages = n_q * top_k * ppb
    assert n_q <= rows and k_new.shape[1] == page
    q_spec = pl.BlockSpec((1, rows, hd), lambda bi, hi, pt, sl: (bi, 0, hi))
    new_spec = pl.BlockSpec((1, page, hd), lambda bi, hi, pt, sl: (bi, 0, hi))
    any_spec = pl.BlockSpec(memory_space=pl.ANY)
    grid_spec = pltpu.PrefetchScalarGridSpec(
        num_scalar_prefetch=2,
        grid=(bsz, heads),
        in_specs=[q_spec, new_spec, new_spec, any_spec, any_spec],
        out_specs=q_spec,
        scratch_shapes=[pltpu.VMEM((2, n_q, top_k * ppb * page, hd), F32),
                        pltpu.VMEM((2, n_q, top_k * ppb * page, hd), F32),
                        pltpu.SemaphoreType.DMA((2, 2, n_pages))],
    )
    return pl.pallas_call(
        functools.partial(_moba_decode_kernel, layer=layer, heads=heads, top_k=top_k,
                          pages_per_block=ppb, n_q=n_q),
        grid_spec=grid_spec,
        out_shape=jax.ShapeDtypeStruct((bsz, rows, d), F32),
        compiler_params=_params("arbitrary", "arbitrary"),
        name="moba_decode",
    )(page_table, sel.reshape(-1), q, k_new, v_new, cache_k, cache_v)


def kernel(x_prompt, x_sample, state_gla, cache_k, cache_v, page_table, norm_mix, norm_ffn, norm_final,
           gla_w_in, gla_w_gate, gla_b_gate, gla_head_norm, gla_w_out, moba_w_qkv, moba_w_o,
           ffn_w_gate_up, ffn_w_down):
    bp, seq, d = x_prompt.shape
    bd, sd, _ = x_sample.shape
    depth = norm_mix.shape[0]
    n_moba, n_pool, page, mh, mhd = cache_k.shape
    past_len = page_table.shape[1] * page
    assert past_len % MOBA_BLOCK == 0 and sd <= min(MOBA_BLOCK, 8) and GLA_CHUNK % sd == 0
    gh, gdk, gdv = state_gla.shape[2:]

    xp = x_prompt.reshape(bp * seq, d)
    xs = x_sample.reshape(bd * sd, d)
    rope_p = _rope_tables(seq, 0, seq, mhd)
    rope_s = _rope_tables(bd * sd, past_len, sd, mhd)
    kmean = {}
    pad_s = GLA_CHUNK - sd

    def pad_tokens(a, n):
        c = a.shape[1]
        return jnp.pad(a.reshape(bd, sd, c), ((0, 0), (0, n), (0, 0))).reshape(bd * (sd + n), c)

    gla_sp, gla_ss, kp_l, vp_l, ks_l, vs_l = [], [], [], [], [], []
    for i in range(depth):
        j = i // 2
        hp = _rmsnorm(xp, norm_mix[i], BF16)
        hs = _rmsnorm(xs, norm_mix[i], BF16)
        if i % 2 == 0:
            n_main = 2 * gh * gdk + 2 * gh * gdv
            proj_p, proj_s = _matmul(hp, hs, gla_w_in, j, n_main)
            g_p = _gla_gate(hp, gla_w_in, j, n_main, gla_w_gate[j], gla_b_gate[j])
            g_s = _gla_gate(hs, gla_w_in, j, n_main, gla_w_gate[j], gla_b_gate[j])
            assert i + 1 < depth and j < n_moba
            op, sp, kmean[j] = _gla_core(proj_p, g_p, gla_head_norm[j], None, j, (bp, gh, gdk, gdv), seq, seq,
                                         block_means_of=(cache_k, page_table))
            os_pad, ss = _gla_core(pad_tokens(proj_s, pad_s), pad_tokens(g_s, pad_s), gla_head_norm[j],
                                   state_gla, j, (bd, gh, gdk, gdv), sd + pad_s, sd)
            os_ = os_pad.reshape(bd, sd + pad_s, d)[:, :sd].reshape(bd * sd, d)
            xp, xs, ssq_p, ssq_s = _matmul(op, os_, gla_w_out, j, d, res=(xp, xs), with_ssq=True)
            gla_sp.append(sp)
            gla_ss.append(ss)
        else:
            rope = (rope_p, rope_s)
            qp, qs = _matmul(hp, hs, moba_w_qkv, j, d, 0, rope=rope)
            kp, ks = _matmul(hp, hs, moba_w_qkv, j, d, d, rope=rope)
            vp, vs = _matmul(hp, hs, moba_w_qkv, j, d, 2 * d)
            op = _moba_prompt(qp, kp, vp, bp, seq, mh)
            q3 = pad_tokens(qs, DECODE_Q_ROWS - sd).reshape(bd, DECODE_Q_ROWS, d)
            k3, v3 = (pad_tokens(a, page - sd).reshape(bd, page, d) for a in (ks, vs))
            sel = _moba_select(qs.reshape(bd, sd, mh, mhd), kmean[j], sd)
            o3 = _moba_decode(q3, k3, v3, cache_k, cache_v, page_table, sel, j, sd)
            xp, xs, ssq_p, ssq_s = _matmul(op, o3[:, :sd].reshape(bd * sd, d), moba_w_o, j, d,
                                           res=(xp, xs), with_ssq=True)
            kp_l.append(kp)
            vp_l.append(vp)
            ks_l.append(ks)
            vs_l.append(vs)
        xp, xs = _ffn(xp, xs, ssq_p, ssq_s, norm_ffn[i], ffn_w_gate_up, ffn_w_down, i)

    y_prompt = _rmsnorm(xp, norm_final, F32).reshape(bp, seq, d)
    y_sample = _rmsnorm(xs, norm_final, F32).reshape(bd, sd, d)
    def pages(rows_per_layer, lead):
        return jnp.stack(rows_per_layer).reshape(len(rows_per_layer), *lead, mh, mhd)

    return (y_prompt, y_sample, jnp.stack(gla_sp), jnp.stack(gla_ss),
            pages(kp_l, (bp * seq // page, page)), pages(vp_l, (bp * seq // page, page)),
            pages(ks_l, (bd, sd)), pages(vs_l, (bd, sd)))
```
